```python
import numpy as np
import jax
import jax.numpy as jnp
from jax import lax

D_MODEL = 2048
BATCH = 4
SEQ = 4096
DEPTH = 2

HEAD_DIM = 64
MIX_WIDTH = D_MODEL
RW_WIDTH = MIX_WIDTH // 2
AT_WIDTH = MIX_WIDTH - RW_WIDTH
RW_HEADS = RW_WIDTH // HEAD_DIM
AT_HEADS = AT_WIDTH // HEAD_DIM
GQA_GROUP = 4
AT_KV_HEADS = AT_HEADS // GQA_GROUP
KV_WIDTH = AT_KV_HEADS * HEAD_DIM
DECAY_RANK = max(32, int(round(1.8 * D_MODEL ** 0.5 / 32)) * 32)
ICLR_RANK = DECAY_RANK
WINDOW = 128
BLOCK = 128
ROPE_THETA = 10000.0
NORM_EPS = 1e-6
LNX_EPS = 64e-5
RW_SHIFT_WIDTH = 3 * RW_WIDTH + DECAY_RANK + ICLR_RANK
IN_COLS = RW_SHIFT_WIDTH + RW_WIDTH + AT_WIDTH + 2 * KV_WIDTH + AT_WIDTH

kernel_name = "hybrid_rwkv7_swa_sink_encoder"


def rms_norm(x, g):
    xf = x.astype(jnp.float32)
    y = xf * lax.rsqrt(jnp.mean(xf * xf, axis=-1, keepdims=True) + NORM_EPS)
    return (y * g.astype(jnp.float32)).astype(x.dtype)


def bidir_token_shift(u, mu):
    prev = jnp.pad(u[:, :-1], ((0, 0), (1, 0), (0, 0)))
    nxt = jnp.pad(u[:, 1:], ((0, 0), (0, 1), (0, 0)))
    return u + mu[0] * (prev - u) + mu[1] * (nxt - u)


def rope(t, positions):
    half = HEAD_DIM // 2
    inv = jnp.power(ROPE_THETA, -jnp.arange(half, dtype=jnp.float32) / half)
    ang = positions.astype(jnp.float32)[:, :, None] * inv
    cos = jnp.cos(ang)[:, :, None, :]
    sin = jnp.sin(ang)[:, :, None, :]
    tf = t.astype(jnp.float32)
    t1, t2 = tf[..., :half], tf[..., half:]
    return jnp.concatenate([t1 * cos - t2 * sin, t2 * cos + t1 * sin], axis=-1)


def wkv_scan(r, decay, k, v, kk, a, reverse):
    B, T, H, N = r.shape

    def step(S, inp):
        r_t, w_t, k_t, v_t, kk_t, a_t = inp
        s_kk = jnp.einsum('bhvk,bhk->bhv', S, kk_t)
        S = (S * w_t[:, :, None, :]
             - s_kk[..., None] * (a_t * kk_t)[:, :, None, :]
             + v_t[..., None] * k_t[:, :, None, :])
        return S, jnp.einsum('bhvk,bhk->bhv', S, r_t)

    xs = tuple(jnp.moveaxis(t, 1, 0) for t in (r, decay, k, v, kk, a))
    S0 = jnp.zeros((B, H, N, N), jnp.float32)
    _, y = lax.scan(step, S0, xs, reverse=reverse)
    return jnp.moveaxis(y, 0, 1)


def rwkv7_branch(r, k, v, xw, xa, w0, decay_up, a0, iclr_up, k_k, k_a, r_k, lnx_g, lnx_b):
    B, T, _ = r.shape
    f32 = jnp.float32
    heads = lambda t: t.astype(f32).reshape(B, T, RW_HEADS, HEAD_DIM)
    rf, kf, vf = r.astype(f32), k.astype(f32), v.astype(f32)
    kk = heads(kf * k_k.astype(f32))
    kk = kk * lax.rsqrt(jnp.maximum(jnp.sum(kk * kk, -1, keepdims=True), 1e-24))
    lw = jnp.tanh(xw.astype(f32))
    xaf = xa.astype(f32)
    rh, vh = heads(rf), heads(vf)
    rkf = r_k.astype(f32)
    y = jnp.zeros_like(rh)
    bonus = jnp.zeros_like(rh[..., :1])
    for d, rev in enumerate((False, True)):
        w = -jax.nn.softplus(-(w0[d].astype(f32) + lw @ decay_up[d].astype(f32))) - 0.5
        decay = jnp.exp(-jnp.exp(w))
        a = jax.nn.sigmoid(a0[d].astype(f32) + xaf @ iclr_up[d].astype(f32))
        kd = heads(kf * (1.0 + (a - 1.0) * k_a.astype(f32)))
        y = y + wkv_scan(rh, heads(decay), kd, vh, kk, heads(a), rev)
        bonus = bonus + jnp.sum(rh * kd * rkf, -1, keepdims=True)
    mu = jnp.mean(y, -1, keepdims=True)
    var = jnp.mean(jnp.square(y - mu), -1, keepdims=True)
    yn = (y - mu) * lax.rsqrt(var + LNX_EPS)
    yn = yn * lnx_g.astype(f32).reshape(RW_HEADS, HEAD_DIM) + lnx_b.astype(f32).reshape(RW_HEADS, HEAD_DIM)
    out = yn + bonus * vh
    return out.reshape(B, T, RW_WIDTH).astype(r.dtype)


def window_attention(q, k, v, sink):
    B, T, _, Dh = q.shape
    nb = T // BLOCK
    qb = (q * (Dh ** -0.5)).reshape(B, nb, BLOCK, AT_KV_HEADS, GQA_GROUP, Dh)

    def band(t):
        tp = jnp.pad(t, ((0, 0), (BLOCK, BLOCK), (0, 0), (0, 0)))
        tp = tp.reshape(B, nb + 2, BLOCK, AT_KV_HEADS, Dh)
        return jnp.concatenate([tp[:, :-2], tp[:, 1:-1], tp[:, 2:]], axis=2)

    kb, vb = band(k), band(v)
    s = jnp.einsum('bnqhgd,bnkhd->bnhgqk', qb, kb).astype(jnp.float32)
    qi = jnp.arange(nb)[:, None, None] * BLOCK + jnp.arange(BLOCK)[None, :, None]
    kj = jnp.arange(nb)[:, None, None] * BLOCK - BLOCK + jnp.arange(3 * BLOCK)[None, None, :]
    mask = (jnp.abs(kj - qi) <= WINDOW) & (kj >= 0) & (kj < T)
    s = jnp.where(mask[None, :, None, None], s, -jnp.inf)
    sk = sink.astype(jnp.float32).reshape(AT_KV_HEADS, GQA_GROUP)[None, None, :, :, None, None]
    m = jnp.maximum(jnp.max(s, -1, keepdims=True), sk)
    p = jnp.exp(s - m)
    denom = jnp.sum(p, -1, keepdims=True) + jnp.exp(sk - m)
    o = jnp.einsum('bnhgqk,bnkhd->bnqhgd', p / denom, vb.astype(jnp.float32))
    return o.reshape(B, T, AT_HEADS * Dh)


def setup_inputs(seed: int = 0) -> dict:
    key = jax.random.key(seed)
    ks = jax.random.split(key, 18)
    f32 = jnp.float32
    nrm = lambda k, shape, s: jax.random.normal(k, shape, f32) * s
    x = nrm(ks[0], (BATCH, SEQ, D_MODEL), 1.0)
    positions = (jnp.arange(SEQ, dtype=jnp.int32)[None, :]
                 + jax.random.randint(ks[1], (BATCH, 1), 0, 1024, dtype=jnp.int32))
    norm_g = 1.0 + nrm(ks[2], (DEPTH, D_MODEL), 0.02)
    w_in = nrm(ks[3], (DEPTH, D_MODEL, IN_COLS), D_MODEL ** -0.5)
    shift_mu = jax.random.uniform(ks[4], (DEPTH, 2, RW_SHIFT_WIDTH), f32, 0.0, 0.5)
    profile = jnp.tile(-6.5 + 5.0 * jnp.linspace(0.0, 1.0, HEAD_DIM, dtype=f32) ** 0.85, RW_HEADS)
    w0 = profile + nrm(ks[5], (DEPTH, 2, RW_WIDTH), 0.1)
    decay_up = nrm(ks[6], (DEPTH, 2, DECAY_RANK, RW_WIDTH), 0.5 * DECAY_RANK ** -0.5)
    a0 = nrm(ks[7], (DEPTH, 2, RW_WIDTH), 0.1)
    iclr_up = nrm(ks[8], (DEPTH, 2, ICLR_RANK, RW_WIDTH), 0.5 * ICLR_RANK ** -0.5)
    k_k = 0.85 + nrm(ks[9], (DEPTH, RW_WIDTH), 0.02)
    k_a = 1.0 + nrm(ks[10], (DEPTH, RW_WIDTH), 0.02)
    r_k = nrm(ks[11], (DEPTH, RW_HEADS, HEAD_DIM), 0.1)
    lnx_g = 1.0 + nrm(ks[12], (DEPTH, RW_WIDTH), 0.02)
    lnx_b = nrm(ks[13], (DEPTH, RW_WIDTH), 0.02)
    sink = nrm(ks[14], (DEPTH, AT_HEADS), 0.5)
    w_out = nrm(ks[15], (DEPTH, MIX_WIDTH, D_MODEL), MIX_WIDTH ** -0.5)
    final_g = 1.0 + nrm(ks[16], (D_MODEL,), 0.02)
    return {"x": x, "positions": positions, "norm_g": norm_g, "w_in": w_in, "shift_mu": shift_mu,
            "w0": w0, "decay_up": decay_up, "a0": a0, "iclr_up": iclr_up, "k_k": k_k, "k_a": k_a,
            "r_k": r_k, "lnx_g": lnx_g, "lnx_b": lnx_b, "sink": sink, "w_out": w_out,
            "final_g": final_g}


def reference(x, positions, norm_g, w_in, shift_mu, w0, decay_up, a0, iclr_up, k_k, k_a, r_k,
              lnx_g, lnx_b, sink, w_out, final_g):
    B, T, _ = x.shape
    col_splits = np.cumsum([RW_SHIFT_WIDTH, RW_WIDTH, AT_WIDTH, KV_WIDTH, KV_WIDTH]).tolist()
    rw_splits = np.cumsum([RW_WIDTH, RW_WIDTH, RW_WIDTH, DECAY_RANK]).tolist()
    for l in range(DEPTH):
        h = rms_norm(x, norm_g[l])
        proj = jnp.einsum('btd,dc->btc', h, w_in[l])
        rw_in, g_rw, q, k_at, v_at, g_at = jnp.split(proj, col_splits, axis=-1)
        rw_in = bidir_token_shift(rw_in, shift_mu[l])
        r, k, v, xw, xa = jnp.split(rw_in, rw_splits, axis=-1)
        y_rw = rwkv7_branch(r, k, v, xw, xa, w0[l], decay_up[l], a0[l], iclr_up[l],
                            k_k[l], k_a[l], r_k[l], lnx_g[l], lnx_b[l])
        qh = rope(q.reshape(B, T, AT_HEADS, HEAD_DIM), positions)
        kh = rope(k_at.reshape(B, T, AT_KV_HEADS, HEAD_DIM), positions)
        vh = v_at.reshape(B, T, AT_KV_HEADS, HEAD_DIM)
        y_at = window_attention(qh, kh, vh, sink[l]).astype(x.dtype)
        mix = jnp.concatenate([y_rw * jax.nn.silu(g_rw), y_at * jax.nn.silu(g_at)], axis=-1)
        x = x + jnp.einsum('btc,cd->btd', mix, w_out[l])
    return rms_norm(x, final_g)
```

```python
import functools

import numpy as np
import jax
import jax.numpy as jnp
from jax import lax
from jax.experimental import pallas as pl
from jax.experimental.pallas import tpu as pltpu

HEAD_DIM = 64
WINDOW = 128
BLOCK = 128
ROPE_THETA = 10000.0
NORM_EPS = 1e-6
LNX_EPS = 64e-5
KK_EPS = 1e-24
DECAY_BIAS = 0.5

LANE = 128
SUBLANE = 8
MXU_DIM = 256
VMEM_LIMIT_BYTES = 56 * 1024 * 1024

CHUNK = 64
SLAB = MXU_DIM
RANK_PAD = LANE

_F32 = jnp.float32
_BF16 = jnp.bfloat16
_NN = (((1,), (0,)), ((), ()))
_NT = (((1,), (1,)), ((), ()))
_TN = (((0,), (0,)), ((), ()))


def _bdot(a, b, dims=_NN):
    return lax.dot_general(a.astype(_BF16), b.astype(_BF16), dims, preferred_element_type=_F32)


def _split2(x):
    hi = x.astype(_BF16)
    lo = (x - hi.astype(_F32)).astype(_BF16)
    return hi, lo


def _split3(x):
    hi = x.astype(_BF16)
    r1 = x - hi.astype(_F32)
    mid = r1.astype(_BF16)
    lo = (r1 - mid.astype(_F32)).astype(_BF16)
    return hi, mid, lo


def _dot_x3(a, b):
    ah, al = _split2(a)
    bh, bl = _split2(b)
    return _bdot(ah, bh) + (_bdot(ah, bl) + _bdot(al, bh))


def _dot_exact_rhs(a_bf16, x, parts):
    out = None
    for p in parts(x):
        t = _bdot(a_bf16, p)
        out = t if out is None else out + t
    return out


def _cparams(semantics):
    return pltpu.CompilerParams(dimension_semantics=semantics, vmem_limit_bytes=VMEM_LIMIT_BYTES)


def _rope_table_kernel(pos_ref, inv_ref, cos_ref, sin_ref):
    ang = pos_ref[...].astype(_F32) * inv_ref[...]
    lane = lax.broadcasted_iota(jnp.int32, ang.shape, 1)
    first_half = (lane & (HEAD_DIM - 1)) < (HEAD_DIM // 2)
    s = jnp.sin(ang)
    cos_ref[...] = jnp.cos(ang)
    sin_ref[...] = jnp.where(first_half, -s, s)


def _rope_tables(positions, m):
    half = HEAD_DIM // 2
    inv = jnp.power(ROPE_THETA, -jnp.arange(half, dtype=_F32) / half)
    inv_row = jnp.tile(inv, LANE // half).reshape(1, LANE)
    tb = min(m, 2048)
    return pl.pallas_call(
        _rope_table_kernel,
        grid=(m // tb,),
        in_specs=[pl.BlockSpec((tb, 1), lambda i: (i, 0)), pl.BlockSpec((1, LANE), lambda i: (0, 0))],
        out_specs=[pl.BlockSpec((tb, LANE), lambda i: (i, 0))] * 2,
        out_shape=[jax.ShapeDtypeStruct((m, LANE), _F32)] * 2,
        compiler_params=_cparams(("arbitrary",)),
        name="rope_tables",
    )(positions.reshape(m, 1), inv_row)


def _rope(t, cos, sin):
    w = t.shape[1]
    half = HEAD_DIM // 2
    lane = lax.broadcasted_iota(jnp.int32, t.shape, 1)
    first_half = (lane & (HEAD_DIM - 1)) < half
    partner = jnp.where(first_half, pltpu.roll(t, w - half, 1), pltpu.roll(t, half, 1))
    reps = w // LANE
    return t * jnp.tile(cos, (1, reps)) + partner * jnp.tile(sin, (1, reps))


def _inproj_kernel(x_ref, g_ref, w_ref, o_ref, h_ref):
    @pl.when(pl.program_id(1) == 0)
    def _():
        rows = min(x_ref.shape[0], 256)
        for r0 in range(0, x_ref.shape[0], rows):
            x = x_ref[r0:r0 + rows, :]
            y = x * lax.rsqrt(jnp.mean(x * x, axis=-1, keepdims=True) + NORM_EPS)
            h_ref[r0:r0 + rows, :] = (y * g_ref[...]).astype(_BF16)

    o_ref[...] = jnp.dot(h_ref[...], w_ref[...], preferred_element_type=_F32)


def _inproj(x2, g, w, tm, tn):
    m, d = x2.shape
    n = w.shape[1]
    return pl.pallas_call(
        _inproj_kernel,
        grid=(m // tm, n // tn),
        in_specs=[
            pl.BlockSpec((tm, d), lambda i, j: (i, 0)),
            pl.BlockSpec((1, d), lambda i, j: (0, 0)),
            pl.BlockSpec((d, tn), lambda i, j: (0, j)),
        ],
        out_specs=pl.BlockSpec((tm, tn), lambda i, j: (i, j)),
        out_shape=jax.ShapeDtypeStruct((m, n), _F32),
        scratch_shapes=[pltpu.VMEM((tm, d), _BF16)],
        compiler_params=_cparams(("arbitrary", "arbitrary")),
        name="norm_inproj",
    )(x2, g.reshape(1, d), w)


def _prep_kernel(rkv_ref, rkvp_ref, rkvn_ref, xw_ref, xwp_ref, xwn_ref, murkv_ref, mux_ref,
                 w0_ref, dup_ref, a0_ref, iup_ref, kk_ref, ka_ref, rk_ref, g_ref,
                 r_out, v_out, kap_out, kd_out, be_out, lw_out, bv_out, *, tb, seq, width):
    i = pl.program_id(0)
    t0 = lax.rem(i * tb, seq)
    first = t0 == 0
    last = t0 + tb == seq
    row = lax.broadcasted_iota(jnp.int32, (tb, 1), 0)
    is_first_row = row == 0
    is_last_row = row == tb - 1
    gmat = g_ref[...]

    def shifted(main, prv, nxt, mu, lo, hi):
        u = main[:, lo:hi]
        p_row = jnp.where(first, 0.0, prv[SUBLANE - 1:SUBLANE, lo:hi])
        n_row = jnp.where(last, 0.0, nxt[0:1, lo:hi])
        up = jnp.where(is_first_row, p_row, pltpu.roll(u, 1, 0))
        un = jnp.where(is_last_row, n_row, pltpu.roll(u, tb - 1, 0))
        return u + mu[0:1, lo:hi] * (up - u) + mu[1:2, lo:hi] * (un - u)

    def group_sum(z):
        zh, zl = _split2(z)
        return _bdot(zh, gmat) + _bdot(zl, gmat)

    xs = shifted(xw_ref, xwp_ref, xwn_ref, mux_ref, 0, 2 * RANK_PAD)
    lw_in = jnp.tanh(xs[:, :RANK_PAD])
    xa = xs[:, RANK_PAD:]

    for s in range(width // SLAB):
        lo, hi = s * SLAB, (s + 1) * SLAB
        r = shifted(rkv_ref, rkvp_ref, rkvn_ref, murkv_ref, lo, hi)
        k = shifted(rkv_ref, rkvp_ref, rkvn_ref, murkv_ref, width + lo, width + hi)
        v = shifted(rkv_ref, rkvp_ref, rkvn_ref, murkv_ref, 2 * width + lo, 2 * width + hi)
        kkv = k * kk_ref[:, lo:hi]
        ss = group_sum(kkv * kkv)
        kap = kkv * lax.rsqrt(jnp.maximum(ss, KK_EPS))
        kd_sum = None
        for d in range(2):
            wlin = w0_ref[d:d + 1, lo:hi] + _dot_x3(lw_in, dup_ref[d, :, lo:hi])
            z = -wlin
            softplus = jnp.maximum(z, 0.0) + jnp.log1p(jnp.exp(-jnp.abs(z)))
            w = -softplus - DECAY_BIAS
            lw_out[d, :, lo:hi] = -jnp.exp(w)
            a = jax.nn.sigmoid(a0_ref[d:d + 1, lo:hi] + _dot_x3(xa, iup_ref[d, :, lo:hi]))
            kd = k * (1.0 + (a - 1.0) * ka_ref[:, lo:hi])
            kd_out[d, :, lo:hi] = kd
            be_out[d, :, lo:hi] = a * kap
            kd_sum = kd if kd_sum is None else kd_sum + kd
        bonus = group_sum(r * kd_sum * rk_ref[:, lo:hi])
        r_out[:, lo:hi] = r
        v_out[:, lo:hi] = v
        kap_out[:, lo:hi] = kap
        bv_out[:, lo:hi] = bonus * v


def _prep(proj, mu_rkv, mu_x, w0, dup, a0, iup, k_k, k_a, r_k, gmat, *, seq, width, xcol_block, tb):
    m = proj.shape[0]
    nsub = tb // SUBLANE
    last_sub = m // SUBLANE - 1
    w3 = 3 * width
    full = lambda shape: pl.BlockSpec(shape, lambda i: (0,) * len(shape))
    prev_idx = lambda i: jnp.maximum(i * nsub - 1, 0)
    next_idx = lambda i: jnp.minimum((i + 1) * nsub, last_sub)
    in_specs = [
        pl.BlockSpec((tb, w3), lambda i: (i, 0)),
        pl.BlockSpec((SUBLANE, w3), lambda i: (prev_idx(i), 0)),
        pl.BlockSpec((SUBLANE, w3), lambda i: (next_idx(i), 0)),
        pl.BlockSpec((tb, 2 * RANK_PAD), lambda i: (i, xcol_block)),
        pl.BlockSpec((SUBLANE, 2 * RANK_PAD), lambda i: (prev_idx(i), xcol_block)),
        pl.BlockSpec((SUBLANE, 2 * RANK_PAD), lambda i: (next_idx(i), xcol_block)),
        full((2, w3)), full((2, 2 * RANK_PAD)),
        full((2, width)), full((2, RANK_PAD, width)), full((2, width)), full((2, RANK_PAD, width)),
        full((1, width)), full((1, width)), full((1, width)), full((SLAB, SLAB)),
    ]
    tok = pl.BlockSpec((tb, width), lambda i: (i, 0))
    tok2 = pl.BlockSpec((2, tb, width), lambda i: (0, i, 0))
    sds = jax.ShapeDtypeStruct((m, width), _F32)
    sds2 = jax.ShapeDtypeStruct((2, m, width), _F32)
    return pl.pallas_call(
        functools.partial(_prep_kernel, tb=tb, seq=seq, width=width),
        grid=(m // tb,),
        in_specs=in_specs,
        out_specs=[tok, tok, tok, tok2, tok2, tok2, tok],
        out_shape=[sds, sds, sds, sds2, sds2, sds2, sds],
        compiler_params=_cparams(("arbitrary",)),
        name="rwkv_prep",
    )(proj, proj, proj, proj, proj, proj, mu_rkv, mu_x, w0, dup, a0, iup, k_k, k_a, r_k, gmat)


_N_LEVELS = int(np.log2(CHUNK)) - 1
_MASK_INCL, _MASK_STRICT, _MASK_BLK2, _MASK_OFF0 = 0, 1, 2, 3


def _scan_masks():
    ri, cj = np.indices((CHUNK, CHUNK))
    out = np.zeros((2, 3 + _N_LEVELS, CHUNK, CHUNK), np.float32)
    for d in range(2):
        diff = (ri - cj) * (1 - 2 * d)
        out[d, _MASK_INCL] = diff >= 0
        out[d, _MASK_STRICT] = diff > 0
        out[d, _MASK_BLK2] = (ri >> 1) == (cj >> 1)
        for lv in range(_N_LEVELS):
            k = lv + 1
            out[d, _MASK_OFF0 + lv] = ((ri >> (k + 1)) == (cj >> (k + 1))) & ((ri >> k) != (cj >> k))
    return out


def _unit_tri_inverse_minus_eye(lmat, blk2, offs):
    m = -jnp.where(blk2, lmat, 0.0)
    for off in offs:
        loff = jnp.where(off, lmat, 0.0)
        q = loff + _bdot(m, loff)
        m = m - q - _bdot(q, m)
    return m


def _scan_kernel(mask_ref, r_ref, v_ref, kap_ref, kd_ref, be_ref, lw_ref, y_ref, st_ref, *, heads):
    c = CHUNK
    d = pl.program_id(0)

    @pl.when(pl.program_id(2) == 0)
    def _():
        st_ref[...] = jnp.zeros_like(st_ref)

    incl = mask_ref[_MASK_INCL] > 0
    strict = mask_ref[_MASK_STRICT] > 0
    blk2 = mask_ref[_MASK_BLK2] > 0
    offs = [mask_ref[_MASK_OFF0 + lv] > 0 for lv in range(_N_LEVELS)]
    tri = mask_ref[_MASK_INCL].astype(_BF16)
    states = [st_ref[h] for h in range(heads)]

    lw = lw_ref[...]
    ci = _dot_exact_rhs(tri, lw, _split3)
    ce = ci - lw
    tot = jnp.where(d == 0, ci[c - 1:c, :], ci[0:1, :])
    g_in = jnp.exp(ci)
    g_inv = jnp.exp(-ci)
    g_tail = jnp.exp(tot - ci)
    kap = kap_ref[...]
    kd = kd_ref[...]
    be = be_ref[...]
    kt_all = kap * jnp.exp(ce)
    rt_all = r_ref[...] * g_in
    kh_all = kd * g_inv
    bh_all = be * g_inv
    kb_all = kd * g_tail
    bb_all = be * g_tail
    v_all = v_ref[...]
    g_col = jnp.broadcast_to(jnp.exp(tot), (LANE, tot.shape[1])).T

    hs = range(heads)
    sl = [slice(h * HEAD_DIM, (h + 1) * HEAD_DIM) for h in hs]
    kt = [kt_all[:, s] for s in sl]
    rt = [rt_all[:, s] for s in sl]
    vh = [v_all[:, s] for s in sl]
    kr = [jnp.concatenate([kt[h], rt[h]], axis=0) for h in hs]
    a_k = [_bdot(kr[h], kh_all[:, sl[h]], _NT) for h in hs]
    a_b = [_bdot(kr[h], bh_all[:, sl[h]], _NT) for h in hs]
    a_kk = [jnp.where(strict, a_k[h][:c], 0.0) for h in hs]
    a_rk = [jnp.where(incl, a_k[h][c:], 0.0) for h in hs]
    lmat = [jnp.where(strict, a_b[h][:c], 0.0) for h in hs]
    a_rb = [jnp.where(incl, a_b[h][c:], 0.0) for h in hs]
    p = [_bdot(a_kk[h], vh[h]) for h in hs]
    y_kv = [_bdot(a_rk[h], vh[h]) for h in hs]
    dst_kv = [_bdot(kb_all[:, sl[h]], vh[h], _TN) for h in hs]

    minv = [-jnp.where(blk2, lmat[h], 0.0) for h in hs]
    for off in offs:
        loff = [jnp.where(off, lmat[h], 0.0) for h in hs]
        q = [loff[h] + _bdot(minv[h], loff[h]) for h in hs]
        minv = [minv[h] - q[h] - _bdot(q[h], minv[h]) for h in hs]

    wt = [kt[h] + _bdot(minv[h], kt[h]) for h in hs]
    ut = [p[h] + _bdot(minv[h], p[h]) for h in hs]
    ws_rs = [_bdot(jnp.concatenate([wt[h], rt[h]], axis=0), states[h]) for h in hs]
    u = [ws_rs[h][:c] + ut[h] for h in hs]
    y_u = [_bdot(a_rb[h], u[h]) for h in hs]
    dst_u = [_bdot(bb_all[:, sl[h]], u[h], _TN) for h in hs]

    for h in hs:
        y_ref[:, sl[h]] = ws_rs[h][c:] + y_kv[h] - y_u[h]
        st_ref[h] = g_col[sl[h], :HEAD_DIM] * states[h] + (dst_kv[h] - dst_u[h])


def _scan(masks, r, v, kap, kd, be, lw, *, batch, seq, width):
    m = r.shape[0]
    nc = seq // CHUNK
    heads = width // HEAD_DIM

    def tblock(d, b, c):
        return b * nc + c + d * (nc - 1 - 2 * c)

    shared = pl.BlockSpec((CHUNK, width), lambda d, b, c: (tblock(d, b, c), 0))
    perdir = pl.BlockSpec((None, CHUNK, width), lambda d, b, c: (d, tblock(d, b, c), 0))
    return pl.pallas_call(
        functools.partial(_scan_kernel, heads=heads),
        grid=(2, batch, nc),
        in_specs=[pl.BlockSpec((None,) + masks.shape[1:], lambda d, b, c: (d, 0, 0, 0)),
                  shared, shared, shared, perdir, perdir, perdir],
        out_specs=perdir,
        out_shape=jax.ShapeDtypeStruct((2, m, width), _F32),
        scratch_shapes=[pltpu.VMEM((heads, HEAD_DIM, HEAD_DIM), _F32)],
        compiler_params=_cparams(("arbitrary", "arbitrary", "arbitrary")),
        name="rwkv_scan",
    )(masks, r, v, kap, kd, be, lw)


def _attn_kernel(sink_ref, q_ref, kp_ref, kc_ref, kn_ref, vp_ref, vc_ref, vn_ref,
                 cp_ref, cc_ref, cn_ref, sp_ref, sc_ref, sn_ref, o_ref, *, seq, group, kv_heads):
    n = pl.program_id(1)
    blk = BLOCK
    scale = HEAD_DIM ** -0.5
    q = _rope(q_ref[...], cc_ref[...], sc_ref[...]) * scale
    k3 = jnp.concatenate([_rope(kp_ref[...], cp_ref[...], sp_ref[...]),
                          _rope(kc_ref[...], cc_ref[...], sc_ref[...]),
                          _rope(kn_ref[...], cn_ref[...], sn_ref[...])], axis=0)
    v3 = jnp.concatenate([vp_ref[...], vc_ref[...], vn_ref[...]], axis=0)

    qi = lax.broadcasted_iota(jnp.int32, (group * blk, 3 * blk), 0) & (blk - 1)
    krel = lax.broadcasted_iota(jnp.int32, (group * blk, 3 * blk), 1) - blk
    kpos = krel + n * blk
    mask = (jnp.abs(krel - qi) <= WINDOW) & (kpos >= 0) & (kpos < seq)
    head_of_row = lax.broadcasted_iota(jnp.int32, (group * blk, 1), 0) // blk

    for g in range(kv_heads):
        qs = jnp.concatenate(
            [q[:, (g * group + j) * HEAD_DIM:(g * group + j + 1) * HEAD_DIM] for j in range(group)], axis=0)
        kg = k3[:, g * HEAD_DIM:(g + 1) * HEAD_DIM]
        vg = v3[:, g * HEAD_DIM:(g + 1) * HEAD_DIM]
        s = jnp.where(mask, _bdot(qs, kg, _NT), -jnp.inf)
        sk = jnp.zeros((group * blk, 1), _F32)
        for j in range(group):
            sk = jnp.where(head_of_row == j, sink_ref[g * group + j], sk)
        mx = jnp.maximum(jnp.max(s, axis=-1, keepdims=True), sk)
        p = jnp.exp(s - mx)
        denom = jnp.sum(p, axis=-1, keepdims=True) + jnp.exp(sk - mx)
        o = _bdot(p, vg) / denom
        for j in range(group):
            hq = g * group + j
            o_ref[:, hq * HEAD_DIM:(hq + 1) * HEAD_DIM] = o[j * blk:(j + 1) * blk]


def _attention(proj, cos_t, sin_t, sink_l, *, batch, seq, at_width, kv_width, q_block, k_block, v_block):
    m = proj.shape[0]
    nb = seq // BLOCK
    kv_heads = kv_width // HEAD_DIM
    group = at_width // kv_width
    cur = lambda b, n: b * nb + n
    prv = lambda b, n: b * nb + jnp.maximum(n - 1, 0)
    nxt = lambda b, n: b * nb + jnp.minimum(n + 1, nb - 1)

    def spec(width, rowfn, colblock):
        return pl.BlockSpec((BLOCK, width), lambda b, n: (rowfn(b, n), colblock))

    in_specs = [pl.BlockSpec(memory_space=pltpu.SMEM), spec(at_width, cur, q_block)]
    in_specs += [spec(kv_width, f, k_block) for f in (prv, cur, nxt)]
    in_specs += [spec(kv_width, f, v_block) for f in (prv, cur, nxt)]
    in_specs += [spec(LANE, f, 0) for f in (prv, cur, nxt)] * 2
    return pl.pallas_call(
        functools.partial(_attn_kernel, seq=seq, group=group, kv_heads=kv_heads),
        grid=(batch, nb),
        in_specs=in_specs,
        out_specs=pl.BlockSpec((BLOCK, at_width), lambda b, n: (cur(b, n), 0)),
        out_shape=jax.ShapeDtypeStruct((m, at_width), _F32),
        compiler_params=_cparams(("arbitrary", "arbitrary")),
        name="window_attention",
    )(sink_l, proj, proj, proj, proj, proj, proj, proj, cos_t, cos_t, cos_t, sin_t, sin_t, sin_t)


def _silu(g):
    return g * jax.nn.sigmoid(g)


def _outproj_kernel(yf_ref, yb_ref, bv_ref, grw_ref, yat_ref, gat_ref, x_ref, lg_ref, lb_ref, g_ref,
                    wrw_ref, wat_ref, fg_ref, o_ref, mix_ref, *, width, final):
    gmat = g_ref[...]
    inv_n = 1.0 / HEAD_DIM

    def group_mean(z):
        zh, zl = _split2(z)
        return (_bdot(zh, gmat) + _bdot(zl, gmat)) * inv_n

    for s in range(width // SLAB):
        lo, hi = s * SLAB, (s + 1) * SLAB
        y = yf_ref[:, lo:hi] + yb_ref[:, lo:hi]
        yc = y - group_mean(y)
        var = group_mean(yc * yc)
        yn = yc * lax.rsqrt(var + LNX_EPS) * lg_ref[:, lo:hi] + lb_ref[:, lo:hi]
        mix_ref[:, lo:hi] = ((yn + bv_ref[:, lo:hi]) * _silu(grw_ref[:, lo:hi])).astype(_BF16)
    mix_at = (yat_ref[...] * _silu(gat_ref[...])).astype(_BF16)
    out = x_ref[...] + (jnp.dot(mix_ref[...], wrw_ref[...], preferred_element_type=_F32)
                        + jnp.dot(mix_at, wat_ref[...], preferred_element_type=_F32))
    if final:
        out = out * lax.rsqrt(jnp.mean(out * out, axis=-1, keepdims=True) + NORM_EPS) * fg_ref[...]
    o_ref[...] = out


def _outproj(y2, bv, proj, yat, x2, lnx_g, lnx_b, gmat, w_rw, w_at, final_g, *, width, at_width,
             grw_block, gat_block, final, tm):
    m, d = x2.shape
    full = lambda shape: pl.BlockSpec(shape, lambda i: (0,) * len(shape))
    in_specs = [
        pl.BlockSpec((None, tm, width), lambda i: (0, i, 0)),
        pl.BlockSpec((None, tm, width), lambda i: (1, i, 0)),
        pl.BlockSpec((tm, width), lambda i: (i, 0)),
        pl.BlockSpec((tm, width), lambda i: (i, grw_block)),
        pl.BlockSpec((tm, at_width), lambda i: (i, 0)),
        pl.BlockSpec((tm, at_width), lambda i: (i, gat_block)),
        pl.BlockSpec((tm, d), lambda i: (i, 0)),
        full((1, width)), full((1, width)), full((SLAB, SLAB)),
        full((width, d)), full((at_width, d)), full((1, d)),
    ]
    return pl.pallas_call(
        functools.partial(_outproj_kernel, width=width, final=final),
        grid=(m // tm,),
        in_specs=in_specs,
        out_specs=pl.BlockSpec((tm, d), lambda i: (i, 0)),
        out_shape=jax.ShapeDtypeStruct((m, d), _F32),
        scratch_shapes=[pltpu.VMEM((tm, width), _BF16)],
        compiler_params=_cparams(("arbitrary",)),
        name="gate_outproj",
    )(y2, y2, bv, proj, yat, proj, x2, lnx_g, lnx_b, gmat, w_rw, w_at, final_g)


def _tile_plan(m):
    return dict(inproj_tm=min(m, 1024), inproj_tn=1152, prep_tb=min(m, 256), out_tm=min(m, 256))


def kernel(x, positions, norm_g, w_in, shift_mu, w0, decay_up, a0, iclr_up, k_k, k_a, r_k, lnx_g, lnx_b,
           sink, w_out, final_g):
    batch, seq, d_model = x.shape
    depth = w_in.shape[0]
    width = k_k.shape[-1]
    at_width = sink.shape[-1] * HEAD_DIM
    rank = decay_up.shape[2]
    in_cols = w_in.shape[-1]
    kv_width = (in_cols - (3 * width + 2 * rank) - width - 2 * at_width) // 2
    m = batch * seq
    assert iclr_up.shape[2] == rank and rank <= RANK_PAD
    assert width % SLAB == 0 and at_width == width and kv_width % LANE == 0
    assert seq % BLOCK == 0 and seq % CHUNK == 0 and BLOCK == WINDOW
    plan = _tile_plan(m)
    assert m % plan["inproj_tm"] == 0 and seq % plan["prep_tb"] == 0 and m % plan["out_tm"] == 0

    o_xw = 3 * width
    o_xa = o_xw + rank
    o_grw = o_xa + rank
    o_q = o_grw + width
    o_k = o_q + at_width
    o_v = o_k + kv_width
    o_gat = o_v + kv_width
    pad_cols = lambda a: jnp.pad(a, [(0, 0)] * (a.ndim - 1) + [(0, RANK_PAD - rank)])
    w_perm = jnp.concatenate([
        w_in[..., :o_xw], w_in[..., o_grw:o_q], w_in[..., o_q:o_k], w_in[..., o_gat:],
        w_in[..., o_k:o_v], w_in[..., o_v:o_gat],
        pad_cols(w_in[..., o_xw:o_xa]), pad_cols(w_in[..., o_xa:o_grw])], axis=-1).astype(_BF16)
    n_cols = w_perm.shape[-1]
    assert n_cols % plan["inproj_tn"] == 0
    grw_block = (3 * width) // width
    q_block = grw_block + 1
    gat_block = q_block + 1
    kcol = 3 * width + width + 2 * at_width
    assert kcol % kv_width == 0 and (kcol + 2 * kv_width) % (2 * RANK_PAD) == 0
    k_block = kcol // kv_width
    v_block = k_block + 1
    xcol_block = (kcol + 2 * kv_width) // (2 * RANK_PAD)

    mu_rkv = shift_mu[..., :o_xw]
    mu_x = jnp.concatenate([pad_cols(shift_mu[..., o_xw:o_xa]), pad_cols(shift_mu[..., o_xa:o_grw])], axis=-1)
    pad_rows = lambda a: jnp.pad(a, [(0, 0), (0, 0), (0, RANK_PAD - rank), (0, 0)])
    dup = pad_rows(decay_up)
    iup = pad_rows(iclr_up)
    rk_flat = r_k.reshape(depth, 1, width)
    w_out_bf = w_out.astype(_BF16)

    lane_head = np.arange(SLAB) // HEAD_DIM
    gmat = jnp.asarray(lane_head[:, None] == lane_head[None, :], _BF16)
    masks = jnp.asarray(_scan_masks())

    x2 = x.reshape(m, d_model)
    cos_t, sin_t = _rope_tables(positions, m)
    for l in range(depth):
        proj = _inproj(x2, norm_g[l], w_perm[l], plan["inproj_tm"], plan["inproj_tn"])
        r, v, kap, kd, be, lw, bv = _prep(
            proj, mu_rkv[l], mu_x[l], w0[l], dup[l], a0[l], iup[l], k_k[l].reshape(1, width),
            k_a[l].reshape(1, width), rk_flat[l], gmat, seq=seq, width=width, xcol_block=xcol_block,
            tb=plan["prep_tb"])
        y2 = _scan(masks, r, v, kap, kd, be, lw, batch=batch, seq=seq, width=width)
        yat = _attention(proj, cos_t, sin_t, sink[l], batch=batch, seq=seq, at_width=at_width,
                         kv_width=kv_width, q_block=q_block, k_block=k_block, v_block=v_block)
        x2 = _outproj(y2, bv, proj, yat, x2, lnx_g[l].reshape(1, width), lnx_b[l].reshape(1, width), gmat,
                      w_out_bf[l, :width], w_out_bf[l, width:], final_g.reshape(1, d_model), width=width,
                      at_width=at_width, grw_block=grw_block, gat_block=gat_block,
                      final=(l == depth - 1), tm=plan["out_tm"])
    return x2.reshape(batch, seq, d_model)
```

```python
import functools

import numpy as np
import jax
import jax.numpy as jnp
from jax import lax
from jax.experimental import pallas as pl
from jax.experimental.pallas import tpu as pltpu

HEAD_DIM = 64
WINDOW = 128
BLOCK = 128
ROPE_THETA = 10000.0
NORM_EPS = 1e-6
LNX_EPS = 64e-5
KK_EPS = 1e-24
DECAY_BIAS = 0.5

LANE = 128
SUBLANE = 8
MXU_DIM = 256
VMEM_LIMIT_BYTES = 56 * 1024 * 1024

CHUNK = 64
SLAB = MXU_DIM
RANK_PAD = LANE

_F32 = jnp.float32
_BF16 = jnp.bfloat16
_NN = (((1,), (0,)), ((), ()))
_NT = (((1,), (1,)), ((), ()))
_TN = (((0,), (0,)), ((), ()))


def _bdot(a, b, dims=_NN):
    return lax.dot_general(a.astype(_BF16), b.astype(_BF16), dims, preferred_element_type=_F32)


def _split2(x):
    hi = x.astype(_BF16)
    lo = (x - hi.astype(_F32)).astype(_BF16)
    return hi, lo


def _split3(x):
    hi = x.astype(_BF16)
    r1 = x - hi.astype(_F32)
    mid = r1.astype(_BF16)
    lo = (r1 - mid.astype(_F32)).astype(_BF16)
    return hi, mid, lo


def _dot_x3(a, b):
    ah, al = _split2(a)
    bh, bl = _split2(b)
    return _bdot(ah, bh) + (_bdot(ah, bl) + _bdot(al, bh))


def _dot_exact_rhs(a_bf16, x, parts):
    out = None
    for p in parts(x):
        t = _bdot(a_bf16, p)
        out = t if out is None else out + t
    return out


def _cparams(semantics):
    return pltpu.CompilerParams(dimension_semantics=semantics, vmem_limit_bytes=VMEM_LIMIT_BYTES)


def _rope_table_kernel(pos_ref, inv_ref, cos_ref, sin_ref):
    ang = pos_ref[...].astype(_F32) * inv_ref[...]
    lane = lax.broadcasted_iota(jnp.int32, ang.shape, 1)
    first_half = (lane & (HEAD_DIM - 1)) < (HEAD_DIM // 2)
    s = jnp.sin(ang)
    cos_ref[...] = jnp.cos(ang)
    sin_ref[...] = jnp.where(first_half, -s, s)


def _rope_tables(positions, m):
    half = HEAD_DIM // 2
    inv = jnp.power(ROPE_THETA, -jnp.arange(half, dtype=_F32) / half)
    inv_row = jnp.tile(inv, LANE // half).reshape(1, LANE)
    tb = min(m, 2048)
    return pl.pallas_call(
        _rope_table_kernel,
        grid=(m // tb,),
        in_specs=[pl.BlockSpec((tb, 1), lambda i: (i, 0)), pl.BlockSpec((1, LANE), lambda i: (0, 0))],
        out_specs=[pl.BlockSpec((tb, LANE), lambda i: (i, 0))] * 2,
        out_shape=[jax.ShapeDtypeStruct((m, LANE), _F32)] * 2,
        compiler_params=_cparams(("arbitrary",)),
        name="rope_tables",
    )(positions.reshape(m, 1), inv_row)


def _rope(t, cos, sin):
    w = t.shape[1]
    half = HEAD_DIM // 2
    lane = lax.broadcasted_iota(jnp.int32, t.shape, 1)
    first_half = (lane & (HEAD_DIM - 1)) < half
    partner = jnp.where(first_half, pltpu.roll(t, w - half, 1), pltpu.roll(t, half, 1))
    reps = w // LANE
    return t * jnp.tile(cos, (1, reps)) + partner * jnp.tile(sin, (1, reps))


def _inproj_kernel(x_ref, g_ref, w_ref, o_ref, h_ref):
    @pl.when(pl.program_id(1) == 0)
    def _():
        rows = min(x_ref.shape[0], 256)
        for r0 in range(0, x_ref.shape[0], rows):
            x = x_ref[r0:r0 + rows, :]
            y = x * lax.rsqrt(jnp.mean(x * x, axis=-1, keepdims=True) + NORM_EPS)
            h_ref[r0:r0 + rows, :] = (y * g_ref[...]).astype(_BF16)

    o_ref[...] = jnp.dot(h_ref[...], w_ref[...], preferred_element_type=_F32)


def _inproj(x2, g, w, tm, tn):
    m, d = x2.shape
    n = w.shape[1]
    return pl.pallas_call(
        _inproj_kernel,
        grid=(m // tm, n // tn),
        in_specs=[
            pl.BlockSpec((tm, d), lambda i, j: (i, 0)),
            pl.BlockSpec((1, d), lambda i, j: (0, 0)),
            pl.BlockSpec((d, tn), lambda i, j: (0, j)),
        ],
        out_specs=pl.BlockSpec((tm, tn), lambda i, j: (i, j)),
        out_shape=jax.ShapeDtypeStruct((m, n), _F32),
        scratch_shapes=[pltpu.VMEM((tm, d), _BF16)],
        compiler_params=_cparams(("arbitrary", "arbitrary")),
        name="norm_inproj",
    )(x2, g.reshape(1, d), w)


def _prep_kernel(rkv_ref, rkvp_ref, rkvn_ref, xw_ref, xwp_ref, xwn_ref, murkv_ref, mux_ref,
                 w0_ref, dup_ref, a0_ref, iup_ref, kk_ref, ka_ref, rk_ref, g_ref,
                 r_out, v_out, kap_out, kd_out, be_out, lw_out, bv_out, *, tb, seq, width):
    i = pl.program_id(0)
    t0 = lax.rem(i * tb, seq)
    first = t0 == 0
    last = t0 + tb == seq
    row = lax.broadcasted_iota(jnp.int32, (tb, 1), 0)
    is_first_row = row == 0
    is_last_row = row == tb - 1
    gmat = g_ref[...]

    def shifted(main, prv, nxt, mu, lo, hi):
        u = main[:, lo:hi]
        p_row = jnp.where(first, 0.0, prv[SUBLANE - 1:SUBLANE, lo:hi])
        n_row = jnp.where(last, 0.0, nxt[0:1, lo:hi])
        up = jnp.where(is_first_row, p_row, pltpu.roll(u, 1, 0))
        un = jnp.where(is_last_row, n_row, pltpu.roll(u, tb - 1, 0))
        return u + mu[0:1, lo:hi] * (up - u) + mu[1:2, lo:hi] * (un - u)

    def group_sum(z):
        zh, zl = _split2(z)
        return _bdot(zh, gmat) + _bdot(zl, gmat)

    xs = shifted(xw_ref, xwp_ref, xwn_ref, mux_ref, 0, 2 * RANK_PAD)
    lw_in = jnp.tanh(xs[:, :RANK_PAD])
    xa = xs[:, RANK_PAD:]

    for s in range(width // SLAB):
        lo, hi = s * SLAB, (s + 1) * SLAB
        r = shifted(rkv_ref, rkvp_ref, rkvn_ref, murkv_ref, lo, hi)
        k = shifted(rkv_ref, rkvp_ref, rkvn_ref, murkv_ref, width + lo, width + hi)
        v = shifted(rkv_ref, rkvp_ref, rkvn_ref, murkv_ref, 2 * width + lo, 2 * width + hi)
        kkv = k * kk_ref[:, lo:hi]
        ss = group_sum(kkv * kkv)
        kap = kkv * lax.rsqrt(jnp.maximum(ss, KK_EPS))
        kd_sum = None
        for d in range(2):
            wlin = w0_ref[d:d + 1, lo:hi] + _dot_x3(lw_in, dup_ref[d, :, lo:hi])
            z = -wlin
            softplus = jnp.maximum(z, 0.0) + jnp.log1p(jnp.exp(-jnp.abs(z)))
            w = -softplus - DECAY_BIAS
            lw_out[d, :, lo:hi] = -jnp.exp(w)
            a = jax.nn.sigmoid(a0_ref[d:d + 1, lo:hi] + _dot_x3(xa, iup_ref[d, :, lo:hi]))
            kd = k * (1.0 + (a - 1.0) * ka_ref[:, lo:hi])
            kd_out[d, :, lo:hi] = kd
            be_out[d, :, lo:hi] = a * kap
            kd_sum = kd if kd_sum is None else kd_sum + kd
        bonus = group_sum(r * kd_sum * rk_ref[:, lo:hi])
        r_out[:, lo:hi] = r
        v_out[:, lo:hi] = v
        kap_out[:, lo:hi] = kap
        bv_out[:, lo:hi] = bonus * v


def _prep(proj, mu_rkv, mu_x, w0, dup, a0, iup, k_k, k_a, r_k, gmat, *, seq, width, xcol_block, tb):
    m = proj.shape[0]
    nsub = tb // SUBLANE
    last_sub = m // SUBLANE - 1
    w3 = 3 * width
    full = lambda shape: pl.BlockSpec(shape, lambda i: (0,) * len(shape))
    prev_idx = lambda i: jnp.maximum(i * nsub - 1, 0)
    next_idx = lambda i: jnp.minimum((i + 1) * nsub, last_sub)
    in_specs = [
        pl.BlockSpec((tb, w3), lambda i: (i, 0)),
        pl.BlockSpec((SUBLANE, w3), lambda i: (prev_idx(i), 0)),
        pl.BlockSpec((SUBLANE, w3), lambda i: (next_idx(i), 0)),
        pl.BlockSpec((tb, 2 * RANK_PAD), lambda i: (i, xcol_block)),
        pl.BlockSpec((SUBLANE, 2 * RANK_PAD), lambda i: (prev_idx(i), xcol_block)),
        pl.BlockSpec((SUBLANE, 2 * RANK_PAD), lambda i: (next_idx(i), xcol_block)),
        full((2, w3)), full((2, 2 * RANK_PAD)),
        full((2, width)), full((2, RANK_PAD, width)), full((2, width)), full((2, RANK_PAD, width)),
        full((1, width)), full((1, width)), full((1, width)), full((SLAB, SLAB)),
    ]
    tok = pl.BlockSpec((tb, width), lambda i: (i, 0))
    tok2 = pl.BlockSpec((2, tb, width), lambda i: (0, i, 0))
    sds = jax.ShapeDtypeStruct((m, width), _F32)
    sds2 = jax.ShapeDtypeStruct((2, m, width), _F32)
    return pl.pallas_call(
        functools.partial(_prep_kernel, tb=tb, seq=seq, width=width),
        grid=(m // tb,),
        in_specs=in_specs,
        out_specs=[tok, tok, tok, tok2, tok2, tok2, tok],
        out_shape=[sds, sds, sds, sds2, sds2, sds2, sds],
        compiler_params=_cparams(("arbitrary",)),
        name="rwkv_prep",
    )(proj, proj, proj, proj, proj, proj, mu_rkv, mu_x, w0, dup, a0, iup, k_k, k_a, r_k, gmat)


_N_LEVELS = int(np.log2(CHUNK)) - 1
_MASK_INCL, _MASK_STRICT, _MASK_BLK2, _MASK_OFF0 = 0, 1, 2, 3
PACK = MXU_DIM // HEAD_DIM
assert CHUNK == HEAD_DIM and LANE == 2 * HEAD_DIM


def _scan_masks():
    ri, cj = np.indices((CHUNK, CHUNK))
    out = np.zeros((2, 3 + _N_LEVELS, CHUNK, CHUNK), np.float32)
    for d in range(2):
        diff = (ri - cj) * (1 - 2 * d)
        out[d, _MASK_INCL] = diff >= 0
        out[d, _MASK_STRICT] = diff > 0
        out[d, _MASK_BLK2] = (ri >> 1) == (cj >> 1)
        for lv in range(_N_LEVELS):
            k = lv + 1
            out[d, _MASK_OFF0 + lv] = ((ri >> (k + 1)) == (cj >> (k + 1))) & ((ri >> k) != (cj >> k))
    return np.tile(out, (1, 1, 1, PACK))


def _scan_kernel(mask_ref, bd_ref, rf_ref, vf_ref, kapf_ref, kdf_ref, bef_ref, lwf_ref,
                 rb_ref, vb_ref, kapb_ref, kdb_ref, beb_ref, lwb_ref, yf_ref, yb_ref, st_ref, *, groups, rows):
    c = CHUNK
    gw = PACK * HEAD_DIM

    @pl.when(pl.program_id(1) == 0)
    def _():
        st_ref[...] = jnp.zeros_like(st_ref)

    blk2 = mask_ref[0, _MASK_BLK2] > 0
    offs = [mask_ref[0, _MASK_OFF0 + lv] > 0 for lv in range(_N_LEVELS)]
    bdm = bd_ref[...]
    bd_keep = bdm > 0

    keep_lo = bdm[0:c, 0:LANE]
    keep_hi = bdm[HEAD_DIM:HEAD_DIM + c, 0:LANE]
    zero = jnp.zeros((c, LANE), _BF16)

    def bd(x):
        xb = x.astype(_BF16)
        blocks = []
        for t in range(gw // LANE):
            col = xb[:, t * LANE:(t + 1) * LANE]
            for keep in (keep_lo, keep_hi):
                row = [zero] * (gw // LANE)
                row[t] = col * keep
                blocks.append(jnp.concatenate(row, axis=1))
        return jnp.concatenate(blocks, axis=0)

    dir_refs = ((rf_ref, vf_ref, kapf_ref, kdf_ref, bef_ref, lwf_ref, yf_ref),
                (rb_ref, vb_ref, kapb_ref, kdb_ref, beb_ref, lwb_ref, yb_ref))
    chains = []
    for d, (r_ref, v_ref, kap_ref, kd_ref, be_ref, lw_ref, y_ref) in enumerate(dir_refs):
        incl = mask_ref[d, _MASK_INCL] > 0
        strict = mask_ref[d, _MASK_STRICT] > 0
        tri = mask_ref[d, _MASK_INCL][:, :c].astype(_BF16)
        for b in range(rows):
            lw = lw_ref[b]
            ci = _dot_exact_rhs(tri, lw, _split3)
            ce = ci - lw
            tot = ci[c - 1:c, :] if d == 0 else ci[0:1, :]
            g_inv = jnp.exp(-ci)
            g_tail = jnp.exp(tot - ci)
            kd = kd_ref[b]
            be = be_ref[b]
            kt_all = kap_ref[b] * jnp.exp(ce)
            rt_all = r_ref[b] * jnp.exp(ci)
            kh_all = kd * g_inv
            bh_all = be * g_inv
            kb_all = kd * g_tail
            bb_all = be * g_tail
            v_all = v_ref[b]
            g_col = jnp.broadcast_to(jnp.exp(tot), (LANE, tot.shape[1])).T
            for g in range(groups):
                s = slice(g * gw, (g + 1) * gw)
                chains.append(dict(
                    incl=incl, strict=strict, kt=kt_all[:, s], rt=rt_all[:, s], kh=kh_all[:, s],
                    bh=bh_all[:, s], kb=kb_all[:, s], bb=bb_all[:, s], v=v_all[:, s],
                    decay=jnp.concatenate([g_col[s, :]] * (gw // LANE), axis=1),
                    y_ref=y_ref, b=b, sl=s, st=(d * rows + b) * groups + g))

    n = range(len(chains))
    ch = chains
    states = [st_ref[ch[i]["st"]] for i in n]
    kr = [jnp.concatenate([ch[i]["kt"], ch[i]["rt"]], axis=0) for i in n]
    a_k = [_bdot(kr[i], bd(ch[i]["kh"]), _NT) for i in n]
    a_b = [_bdot(kr[i], bd(ch[i]["bh"]), _NT) for i in n]
    a_kk = [jnp.where(ch[i]["strict"], a_k[i][:c], 0.0) for i in n]
    a_rk = [jnp.where(ch[i]["incl"], a_k[i][c:], 0.0) for i in n]
    lmat = [jnp.where(ch[i]["strict"], a_b[i][:c], 0.0) for i in n]
    a_rb = [jnp.where(ch[i]["incl"], a_b[i][c:], 0.0) for i in n]
    pv = [_bdot(jnp.concatenate([a_kk[i], a_rk[i]], axis=0), bd(ch[i]["v"])) for i in n]
    p = [pv[i][:c] for i in n]
    y_kv = [pv[i][c:] for i in n]

    minv = [-jnp.where(blk2, lmat[i], 0.0) for i in n]
    for off in offs:
        loff = [jnp.where(off, lmat[i], 0.0) for i in n]
        q = [loff[i] + _bdot(minv[i], bd(loff[i])) for i in n]
        minv = [minv[i] - q[i] - _bdot(q[i], bd(minv[i])) for i in n]

    wt = [ch[i]["kt"] + _bdot(minv[i], bd(ch[i]["kt"])) for i in n]
    ut = [p[i] + _bdot(minv[i], bd(p[i])) for i in n]
    ws_rs = [_bdot(jnp.concatenate([wt[i], ch[i]["rt"]], axis=0), states[i]) for i in n]
    u = [ws_rs[i][:c] + ut[i] for i in n]
    y_u = [_bdot(a_rb[i], bd(u[i])) for i in n]
    dst = [_bdot(jnp.concatenate([ch[i]["kb"], -ch[i]["bb"]], axis=0),
                 jnp.concatenate([ch[i]["v"], u[i]], axis=0), _TN) for i in n]

    for i in n:
        ch[i]["y_ref"][ch[i]["b"], :, ch[i]["sl"]] = ws_rs[i][c:] + y_kv[i] - y_u[i]
        st_ref[ch[i]["st"]] = ch[i]["decay"] * states[i] + jnp.where(bd_keep, dst[i], 0.0)


def _scan(masks, bdmask, r, v, kap, kd, be, lw, *, batch, seq, width, rows):
    nc = seq // CHUNK
    gw = PACK * HEAD_DIM
    groups = width // gw
    to3 = lambda a: a.reshape(batch, seq, width)
    to4 = lambda a: a.reshape(2, batch, seq, width)
    fwd = pl.BlockSpec((rows, CHUNK, width), lambda i, c: (i, c, 0))
    bwd = pl.BlockSpec((rows, CHUNK, width), lambda i, c: (i, nc - 1 - c, 0))
    fwd_d = pl.BlockSpec((None, rows, CHUNK, width), lambda i, c: (0, i, c, 0))
    bwd_d = pl.BlockSpec((None, rows, CHUNK, width), lambda i, c: (1, i, nc - 1 - c, 0))
    r3, v3, kap3, kd4, be4, lw4 = to3(r), to3(v), to3(kap), to4(kd), to4(be), to4(lw)
    sds = jax.ShapeDtypeStruct((batch, seq, width), _F32)
    yf, yb = pl.pallas_call(
        functools.partial(_scan_kernel, groups=groups, rows=rows),
        grid=(batch // rows, nc),
        in_specs=[pl.BlockSpec(masks.shape, lambda i, c: (0, 0, 0, 0)),
                  pl.BlockSpec((gw, gw), lambda i, c: (0, 0)),
                  fwd, fwd, fwd, fwd_d, fwd_d, fwd_d, bwd, bwd, bwd, bwd_d, bwd_d, bwd_d],
        out_specs=[fwd, bwd],
        out_shape=[sds, sds],
        scratch_shapes=[pltpu.VMEM((2 * rows * groups, gw, gw), _F32)],
        compiler_params=_cparams(("arbitrary", "arbitrary")),
        name="rwkv_scan",
    )(masks, bdmask, r3, v3, kap3, kd4, be4, lw4, r3, v3, kap3, kd4, be4, lw4)
    return yf.reshape(batch * seq, width), yb.reshape(batch * seq, width)


def _attn_kernel(sink_ref, q_ref, kp_ref, kc_ref, kn_ref, vp_ref, vc_ref, vn_ref,
                 cp_ref, cc_ref, cn_ref, sp_ref, sc_ref, sn_ref, o_ref, *, seq, group, kv_heads):
    n = pl.program_id(1)
    blk = BLOCK
    scale = HEAD_DIM ** -0.5
    q = _rope(q_ref[...], cc_ref[...], sc_ref[...]) * scale
    k3 = jnp.concatenate([_rope(kp_ref[...], cp_ref[...], sp_ref[...]),
                          _rope(kc_ref[...], cc_ref[...], sc_ref[...]),
                          _rope(kn_ref[...], cn_ref[...], sn_ref[...])], axis=0)
    v3 = jnp.concatenate([vp_ref[...], vc_ref[...], vn_ref[...]], axis=0)

    qi = lax.broadcasted_iota(jnp.int32, (group * blk, 3 * blk), 0) & (blk - 1)
    krel = lax.broadcasted_iota(jnp.int32, (group * blk, 3 * blk), 1) - blk
    kpos = krel + n * blk
    mask = (jnp.abs(krel - qi) <= WINDOW) & (kpos >= 0) & (kpos < seq)
    head_of_row = lax.broadcasted_iota(jnp.int32, (group * blk, 1), 0) // blk

    for g in range(kv_heads):
        qs = jnp.concatenate(
            [q[:, (g * group + j) * HEAD_DIM:(g * group + j + 1) * HEAD_DIM] for j in range(group)], axis=0)
        kg = k3[:, g * HEAD_DIM:(g + 1) * HEAD_DIM]
        vg = v3[:, g * HEAD_DIM:(g + 1) * HEAD_DIM]
        s = jnp.where(mask, _bdot(qs, kg, _NT), -jnp.inf)
        sk = jnp.zeros((group * blk, 1), _F32)
        for j in range(group):
            sk = jnp.where(head_of_row == j, sink_ref[g * group + j], sk)
        mx = jnp.maximum(jnp.max(s, axis=-1, keepdims=True), sk)
        p = jnp.exp(s - mx)
        denom = jnp.sum(p, axis=-1, keepdims=True) + jnp.exp(sk - mx)
        o = _bdot(p, vg) / denom
        for j in range(group):
            hq = g * group + j
            o_ref[:, hq * HEAD_DIM:(hq + 1) * HEAD_DIM] = o[j * blk:(j + 1) * blk]


def _attention(proj, cos_t, sin_t, sink_l, *, batch, seq, at_width, kv_width, q_block, k_block, v_block):
    m = proj.shape[0]
    nb = seq // BLOCK
    kv_heads = kv_width // HEAD_DIM
    group = at_width // kv_width
    cur = lambda b, n: b * nb + n
    prv = lambda b, n: b * nb + jnp.maximum(n - 1, 0)
    nxt = lambda b, n: b * nb + jnp.minimum(n + 1, nb - 1)

    def spec(width, rowfn, colblock):
        return pl.BlockSpec((BLOCK, width), lambda b, n: (rowfn(b, n), colblock))

    in_specs = [pl.BlockSpec(memory_space=pltpu.SMEM), spec(at_width, cur, q_block)]
    in_specs += [spec(kv_width, f, k_block) for f in (prv, cur, nxt)]
    in_specs += [spec(kv_width, f, v_block) for f in (prv, cur, nxt)]
    in_specs += [spec(LANE, f, 0) for f in (prv, cur, nxt)] * 2
    return pl.pallas_call(
        functools.partial(_attn_kernel, seq=seq, group=group, kv_heads=kv_heads),
        grid=(batch, nb),
        in_specs=in_specs,
        out_specs=pl.BlockSpec((BLOCK, at_width), lambda b, n: (cur(b, n), 0)),
        out_shape=jax.ShapeDtypeStruct((m, at_width), _F32),
        compiler_params=_cparams(("arbitrary", "arbitrary")),
        name="window_attention",
    )(sink_l, proj, proj, proj, proj, proj, proj, proj, cos_t, cos_t, cos_t, sin_t, sin_t, sin_t)


def _silu(g):
    return g * jax.nn.sigmoid(g)


def _outproj_kernel(yf_ref, yb_ref, bv_ref, grw_ref, yat_ref, gat_ref, x_ref, lg_ref, lb_ref, g_ref,
                    wrw_ref, wat_ref, fg_ref, o_ref, mix_ref, *, width, final):
    gmat = g_ref[...]
    inv_n = 1.0 / HEAD_DIM

    def group_mean(z):
        zh, zl = _split2(z)
        return (_bdot(zh, gmat) + _bdot(zl, gmat)) * inv_n

    for s in range(width // SLAB):
        lo, hi = s * SLAB, (s + 1) * SLAB
        y = yf_ref[:, lo:hi] + yb_ref[:, lo:hi]
        yc = y - group_mean(y)
        var = group_mean(yc * yc)
        yn = yc * lax.rsqrt(var + LNX_EPS) * lg_ref[:, lo:hi] + lb_ref[:, lo:hi]
        mix_ref[:, lo:hi] = ((yn + bv_ref[:, lo:hi]) * _silu(grw_ref[:, lo:hi])).astype(_BF16)
    mix_at = (yat_ref[...] * _silu(gat_ref[...])).astype(_BF16)
    out = x_ref[...] + (jnp.dot(mix_ref[...], wrw_ref[...], preferred_element_type=_F32)
                        + jnp.dot(mix_at, wat_ref[...], preferred_element_type=_F32))
    if final:
        out = out * lax.rsqrt(jnp.mean(out * out, axis=-1, keepdims=True) + NORM_EPS) * fg_ref[...]
    o_ref[...] = out


def _outproj(yf, yb, bv, proj, yat, x2, lnx_g, lnx_b, gmat, w_rw, w_at, final_g, *, width, at_width,
             grw_block, gat_block, final, tm):
    m, d = x2.shape
    full = lambda shape: pl.BlockSpec(shape, lambda i: (0,) * len(shape))
    in_specs = [
        pl.BlockSpec((tm, width), lambda i: (i, 0)),
        pl.BlockSpec((tm, width), lambda i: (i, 0)),
        pl.BlockSpec((tm, width), lambda i: (i, 0)),
        pl.BlockSpec((tm, width), lambda i: (i, grw_block)),
        pl.BlockSpec((tm, at_width), lambda i: (i, 0)),
        pl.BlockSpec((tm, at_width), lambda i: (i, gat_block)),
        pl.BlockSpec((tm, d), lambda i: (i, 0)),
        full((1, width)), full((1, width)), full((SLAB, SLAB)),
        full((width, d)), full((at_width, d)), full((1, d)),
    ]
    return pl.pallas_call(
        functools.partial(_outproj_kernel, width=width, final=final),
        grid=(m // tm,),
        in_specs=in_specs,
        out_specs=pl.BlockSpec((tm, d), lambda i: (i, 0)),
        out_shape=jax.ShapeDtypeStruct((m, d), _F32),
        scratch_shapes=[pltpu.VMEM((tm, width), _BF16)],
        compiler_params=_cparams(("arbitrary",)),
        name="gate_outproj",
    )(yf, yb, bv, proj, yat, proj, x2, lnx_g, lnx_b, gmat, w_rw, w_at, final_g)


def _tile_plan(m, batch):
    return dict(inproj_tm=min(m, 1024), inproj_tn=1152, prep_tb=min(m, 256), out_tm=min(m, 256),
                scan_rows=2 if batch % 2 == 0 else 1)


def kernel(x, positions, norm_g, w_in, shift_mu, w0, decay_up, a0, iclr_up, k_k, k_a, r_k, lnx_g, lnx_b,
           sink, w_out, final_g):
    batch, seq, d_model = x.shape
    depth = w_in.shape[0]
    width = k_k.shape[-1]
    at_width = sink.shape[-1] * HEAD_DIM
    rank = decay_up.shape[2]
    in_cols = w_in.shape[-1]
    kv_width = (in_cols - (3 * width + 2 * rank) - width - 2 * at_width) // 2
    m = batch * seq
    assert iclr_up.shape[2] == rank and rank <= RANK_PAD
    assert width % SLAB == 0 and at_width == width and kv_width % LANE == 0
    assert seq % BLOCK == 0 and seq % CHUNK == 0 and BLOCK == WINDOW
    plan = _tile_plan(m, batch)
    assert m % plan["inproj_tm"] == 0 and seq % plan["prep_tb"] == 0 and m % plan["out_tm"] == 0

    o_xw = 3 * width
    o_xa = o_xw + rank
    o_grw = o_xa + rank
    o_q = o_grw + width
    o_k = o_q + at_width
    o_v = o_k + kv_width
    o_gat = o_v + kv_width
    pad_cols = lambda a: jnp.pad(a, [(0, 0)] * (a.ndim - 1) + [(0, RANK_PAD - rank)])
    w_perm = jnp.concatenate([
        w_in[..., :o_xw], w_in[..., o_grw:o_q], w_in[..., o_q:o_k], w_in[..., o_gat:],
        w_in[..., o_k:o_v], w_in[..., o_v:o_gat],
        pad_cols(w_in[..., o_xw:o_xa]), pad_cols(w_in[..., o_xa:o_grw])], axis=-1).astype(_BF16)
    n_cols = w_perm.shape[-1]
    assert n_cols % plan["inproj_tn"] == 0
    grw_block = (3 * width) // width
    q_block = grw_block + 1
    gat_block = q_block + 1
    kcol = 3 * width + width + 2 * at_width
    assert kcol % kv_width == 0 and (kcol + 2 * kv_width) % (2 * RANK_PAD) == 0
    k_block = kcol // kv_width
    v_block = k_block + 1
    xcol_block = (kcol + 2 * kv_width) // (2 * RANK_PAD)

    mu_rkv = shift_mu[..., :o_xw]
    mu_x = jnp.concatenate([pad_cols(shift_mu[..., o_xw:o_xa]), pad_cols(shift_mu[..., o_xa:o_grw])], axis=-1)
    pad_rows = lambda a: jnp.pad(a, [(0, 0), (0, 0), (0, RANK_PAD - rank), (0, 0)])
    dup = pad_rows(decay_up)
    iup = pad_rows(iclr_up)
    rk_flat = r_k.reshape(depth, 1, width)
    w_out_bf = w_out.astype(_BF16)

    lane_head = np.arange(SLAB) // HEAD_DIM
    gmat = jnp.asarray(lane_head[:, None] == lane_head[None, :], _BF16)
    masks = jnp.asarray(_scan_masks())

    x2 = x.reshape(m, d_model)
    cos_t, sin_t = _rope_tables(positions, m)
    for l in range(depth):
        proj = _inproj(x2, norm_g[l], w_perm[l], plan["inproj_tm"], plan["inproj_tn"])
        r, v, kap, kd, be, lw, bv = _prep(
            proj, mu_rkv[l], mu_x[l], w0[l], dup[l], a0[l], iup[l], k_k[l].reshape(1, width),
            k_a[l].reshape(1, width), rk_flat[l], gmat, seq=seq, width=width, xcol_block=xcol_block,
            tb=plan["prep_tb"])
        yf, yb = _scan(masks, gmat, r, v, kap, kd, be, lw, batch=batch, seq=seq, width=width,
                       rows=plan["scan_rows"])
        yat = _attention(proj, cos_t, sin_t, sink[l], batch=batch, seq=seq, at_width=at_width,
                         kv_width=kv_width, q_block=q_block, k_block=k_block, v_block=v_block)
        x2 = _outproj(yf, yb, bv, proj, yat, x2, lnx_g[l].reshape(1, width), lnx_b[l].reshape(1, width), gmat,
                      w_out_bf[l, :width], w_out_bf[l, width:], final_g.reshape(1, d_model), width=width,
                      at_width=at_width, grw_block=grw_block, gat_block=gat_block,
                      final=(l == depth - 1), tm=plan["out_tm"])
    return x2.reshape(batch, seq, d_model)
```

```python
import functools
import math

import numpy as np
import jax
import jax.numpy as jnp
from jax import lax
from jax.experimental import pallas as pl
from jax.experimental.pallas import tpu as pltpu

HEAD_DIM = 64
WINDOW = 128
BLOCK = 128
ROPE_THETA = 10000.0
NORM_EPS = 1e-6
LNX_EPS = 64e-5
KK_EPS = 1e-24
DECAY_BIAS = 0.5
LOG2E = math.log2(math.e)
EXP_NEG_DECAY_BIAS = math.exp(-DECAY_BIAS)

LANE = 128
SUBLANE = 8
MXU_DIM = 256
VMEM_LIMIT_BYTES = 56 * 1024 * 1024

CHUNK = 64
SLAB = MXU_DIM
RANK_PAD = LANE

_F32 = jnp.float32
_BF16 = jnp.bfloat16
_NN = (((1,), (0,)), ((), ()))
_NT = (((1,), (1,)), ((), ()))
_TN = (((0,), (0,)), ((), ()))


def _bdot(a, b, dims=_NN):
    return lax.dot_general(a.astype(_BF16), b.astype(_BF16), dims, preferred_element_type=_F32)


def _split2(x):
    hi = x.astype(_BF16)
    lo = (x - hi.astype(_F32)).astype(_BF16)
    return hi, lo


def _split3(x):
    hi = x.astype(_BF16)
    r1 = x - hi.astype(_F32)
    mid = r1.astype(_BF16)
    lo = (r1 - mid.astype(_F32)).astype(_BF16)
    return hi, mid, lo


def _dot_exact_rhs(a_bf16, x, parts):
    out = None
    for p in parts(x):
        t = _bdot(a_bf16, p)
        out = t if out is None else out + t
    return out


def _cparams(semantics):
    return pltpu.CompilerParams(dimension_semantics=semantics, vmem_limit_bytes=VMEM_LIMIT_BYTES)


def _rope_table_kernel(pos_ref, inv_ref, cos_ref, sin_ref):
    ang = pos_ref[...].astype(_F32) * inv_ref[...]
    lane = lax.broadcasted_iota(jnp.int32, ang.shape, 1)
    first_half = (lane & (HEAD_DIM - 1)) < (HEAD_DIM // 2)
    s = jnp.sin(ang)
    cos_ref[...] = jnp.cos(ang)
    sin_ref[...] = jnp.where(first_half, -s, s)


def _rope_tables(positions, m):
    half = HEAD_DIM // 2
    inv = jnp.power(ROPE_THETA, -jnp.arange(half, dtype=_F32) / half)
    inv_row = jnp.tile(inv, LANE // half).reshape(1, LANE)
    tb = min(m, 2048)
    return pl.pallas_call(
        _rope_table_kernel,
        grid=(m // tb,),
        in_specs=[pl.BlockSpec((tb, 1), lambda i: (i, 0)), pl.BlockSpec((1, LANE), lambda i: (0, 0))],
        out_specs=[pl.BlockSpec((tb, LANE), lambda i: (i, 0))] * 2,
        out_shape=[jax.ShapeDtypeStruct((m, LANE), _F32)] * 2,
        compiler_params=_cparams(("arbitrary",)),
        name="rope_tables",
    )(positions.reshape(m, 1), inv_row)


def _rope(t, cos, sin):
    w = t.shape[1]
    half = HEAD_DIM // 2
    lane = lax.broadcasted_iota(jnp.int32, t.shape, 1)
    first_half = (lane & (HEAD_DIM - 1)) < half
    partner = jnp.where(first_half, pltpu.roll(t, w - half, 1), pltpu.roll(t, half, 1))
    reps = w // LANE
    return t * jnp.tile(cos, (1, reps)) + partner * jnp.tile(sin, (1, reps))


def _inproj_kernel(x_ref, g_ref, w_ref, o_ref, h_ref):
    @pl.when(pl.program_id(1) == 0)
    def _():
        rows = min(x_ref.shape[0], 256)
        for r0 in range(0, x_ref.shape[0], rows):
            x = x_ref[r0:r0 + rows, :]
            y = x * lax.rsqrt(jnp.mean(x * x, axis=-1, keepdims=True) + NORM_EPS)
            h_ref[r0:r0 + rows, :] = (y * g_ref[...]).astype(_BF16)

    o_ref[...] = jnp.dot(h_ref[...], w_ref[...], preferred_element_type=_F32)


def _inproj(x2, g, w, tm, tn):
    m, d = x2.shape
    n = w.shape[1]
    return pl.pallas_call(
        _inproj_kernel,
        grid=(m // tm, n // tn),
        in_specs=[
            pl.BlockSpec((tm, d), lambda i, j: (i, 0)),
            pl.BlockSpec((1, d), lambda i, j: (0, 0)),
            pl.BlockSpec((d, tn), lambda i, j: (0, j)),
        ],
        out_specs=pl.BlockSpec((tm, tn), lambda i, j: (i, j)),
        out_shape=jax.ShapeDtypeStruct((m, n), _F32),
        scratch_shapes=[pltpu.VMEM((tm, d), _BF16)],
        compiler_params=_cparams(("arbitrary", "arbitrary")),
        name="norm_inproj",
    )(x2, g.reshape(1, d), w)


def _prep_kernel(rkv_ref, rkvp_ref, rkvn_ref, xw_ref, xwp_ref, xwn_ref, murkv_ref, mux_ref,
                 w0_ref, dup_ref, a0_ref, iup_ref, kk_ref, ka_ref, rk_ref, g_ref,
                 r_out, v_out, kap_out, kd_out, be_out, lw_out, bv_out, *, tb, seq, width):
    i = pl.program_id(0)
    t0 = lax.rem(i * tb, seq)
    first = t0 == 0
    last = t0 + tb == seq
    row = lax.broadcasted_iota(jnp.int32, (tb, 1), 0)
    is_first_row = row == 0
    is_last_row = row == tb - 1
    gmat = g_ref[...]

    def shifted(main, prv, nxt, mu, lo, hi):
        u = main[:, lo:hi]
        p_row = jnp.where(first, 0.0, prv[SUBLANE - 1:SUBLANE, lo:hi])
        n_row = jnp.where(last, 0.0, nxt[0:1, lo:hi])
        up = jnp.where(is_first_row, p_row, pltpu.roll(u, 1, 0))
        un = jnp.where(is_last_row, n_row, pltpu.roll(u, tb - 1, 0))
        return u + mu[0:1, lo:hi] * (up - u) + mu[1:2, lo:hi] * (un - u)

    def group_sum(z):
        return _bdot(z, gmat)

    xs = shifted(xw_ref, xwp_ref, xwn_ref, mux_ref, 0, 2 * RANK_PAD)
    lw_hi, lw_lo = _split2(jnp.tanh(xs[:, :RANK_PAD]))
    xa = xs[:, RANK_PAD:].astype(_BF16)

    for s in range(width // SLAB):
        lo, hi = s * SLAB, (s + 1) * SLAB
        r = shifted(rkv_ref, rkvp_ref, rkvn_ref, murkv_ref, lo, hi)
        k = shifted(rkv_ref, rkvp_ref, rkvn_ref, murkv_ref, width + lo, width + hi)
        v = shifted(rkv_ref, rkvp_ref, rkvn_ref, murkv_ref, 2 * width + lo, 2 * width + hi)
        kkv = k * kk_ref[:, lo:hi]
        ss = group_sum(kkv * kkv)
        kap = kkv * lax.rsqrt(jnp.maximum(ss, KK_EPS))
        kd_sum = None
        for d in range(2):
            dup_hi, dup_lo = _split2(dup_ref[d, :, lo:hi])
            wlin = w0_ref[d:d + 1, lo:hi] + (_bdot(lw_hi, dup_hi) + (_bdot(lw_hi, dup_lo) + _bdot(lw_lo, dup_hi)))
            lw_out[d, :, lo:hi] = -EXP_NEG_DECAY_BIAS * jax.nn.sigmoid(wlin)
            a = jax.nn.sigmoid(a0_ref[d:d + 1, lo:hi] + _bdot(xa, iup_ref[d, :, lo:hi]))
            kd = k * (1.0 + (a - 1.0) * ka_ref[:, lo:hi])
            kd_out[d, :, lo:hi] = kd
            be_out[d, :, lo:hi] = a * kap
            kd_sum = kd if kd_sum is None else kd_sum + kd
        bonus = group_sum(r * kd_sum * rk_ref[:, lo:hi])
        r_out[:, lo:hi] = r
        v_out[:, lo:hi] = v
        kap_out[:, lo:hi] = kap
        bv_out[:, lo:hi] = bonus * v


def _prep(proj, mu_rkv, mu_x, w0, dup, a0, iup, k_k, k_a, r_k, gmat, *, seq, width, xcol_block, tb):
    m = proj.shape[0]
    nsub = tb // SUBLANE
    last_sub = m // SUBLANE - 1
    w3 = 3 * width
    full = lambda shape: pl.BlockSpec(shape, lambda i: (0,) * len(shape))
    prev_idx = lambda i: jnp.maximum(i * nsub - 1, 0)
    next_idx = lambda i: jnp.minimum((i + 1) * nsub, last_sub)
    in_specs = [
        pl.BlockSpec((tb, w3), lambda i: (i, 0)),
        pl.BlockSpec((SUBLANE, w3), lambda i: (prev_idx(i), 0)),
        pl.BlockSpec((SUBLANE, w3), lambda i: (next_idx(i), 0)),
        pl.BlockSpec((tb, 2 * RANK_PAD), lambda i: (i, xcol_block)),
        pl.BlockSpec((SUBLANE, 2 * RANK_PAD), lambda i: (prev_idx(i), xcol_block)),
        pl.BlockSpec((SUBLANE, 2 * RANK_PAD), lambda i: (next_idx(i), xcol_block)),
        full((2, w3)), full((2, 2 * RANK_PAD)),
        full((2, width)), full((2, RANK_PAD, width)), full((2, width)), full((2, RANK_PAD, width)),
        full((1, width)), full((1, width)), full((1, width)), full((SLAB, SLAB)),
    ]
    tok = pl.BlockSpec((tb, width), lambda i: (i, 0))
    tok2 = pl.BlockSpec((2, tb, width), lambda i: (0, i, 0))
    sds = jax.ShapeDtypeStruct((m, width), _F32)
    sds2 = jax.ShapeDtypeStruct((2, m, width), _F32)
    return pl.pallas_call(
        functools.partial(_prep_kernel, tb=tb, seq=seq, width=width),
        grid=(m // tb,),
        in_specs=in_specs,
        out_specs=[tok, tok, tok, tok2, tok2, tok2, tok],
        out_shape=[sds, sds, sds, sds2, sds2, sds2, sds],
        compiler_params=_cparams(("arbitrary",)),
        name="rwkv_prep",
    )(proj, proj, proj, proj, proj, proj, mu_rkv, mu_x, w0, dup, a0, iup, k_k, k_a, r_k, gmat)


_N_LEVELS = int(np.log2(CHUNK)) - 1
_MASK_INCL, _MASK_STRICT, _MASK_BLK2, _MASK_OFF0 = 0, 1, 2, 3
PACK = MXU_DIM // HEAD_DIM
assert CHUNK == HEAD_DIM and LANE == 2 * HEAD_DIM


def _scan_masks():
    ri, cj = np.indices((CHUNK, CHUNK))
    out = np.zeros((2, 3 + _N_LEVELS, CHUNK, CHUNK), np.float32)
    for d in range(2):
        diff = (ri - cj) * (1 - 2 * d)
        out[d, _MASK_INCL] = diff >= 0
        out[d, _MASK_STRICT] = diff > 0
        out[d, _MASK_BLK2] = (ri >> 1) == (cj >> 1)
        for lv in range(_N_LEVELS):
            k = lv + 1
            out[d, _MASK_OFF0 + lv] = ((ri >> (k + 1)) == (cj >> (k + 1))) & ((ri >> k) != (cj >> k))
    return np.tile(out, (1, 1, 1, PACK))


def _scan_kernel(mask_ref, bd_ref, rf_ref, vf_ref, kapf_ref, kdf_ref, bef_ref, lwf_ref,
                 rb_ref, vb_ref, kapb_ref, kdb_ref, beb_ref, lwb_ref, yf_ref, yb_ref, st_ref, *, groups, rows):
    c = CHUNK
    gw = PACK * HEAD_DIM

    @pl.when(pl.program_id(1) == 0)
    def _():
        st_ref[...] = jnp.zeros_like(st_ref)

    blk2 = mask_ref[0, _MASK_BLK2] > 0
    offs = [mask_ref[0, _MASK_OFF0 + lv] > 0 for lv in range(_N_LEVELS)]
    bdm = bd_ref[...]
    bd_keep = bdm > 0

    keep_lo = bdm[0:c, 0:LANE]
    keep_hi = bdm[HEAD_DIM:HEAD_DIM + c, 0:LANE]
    zero = jnp.zeros((c, LANE), _BF16)

    def bd(x):
        xb = x.astype(_BF16)
        blocks = []
        for t in range(gw // LANE):
            col = xb[:, t * LANE:(t + 1) * LANE]
            for keep in (keep_lo, keep_hi):
                row = [zero] * (gw // LANE)
                row[t] = col * keep
                blocks.append(jnp.concatenate(row, axis=1))
        return jnp.concatenate(blocks, axis=0)

    dir_refs = ((rf_ref, vf_ref, kapf_ref, kdf_ref, bef_ref, lwf_ref, yf_ref),
                (rb_ref, vb_ref, kapb_ref, kdb_ref, beb_ref, lwb_ref, yb_ref))
    chains = []
    for d, (r_ref, v_ref, kap_ref, kd_ref, be_ref, lw_ref, y_ref) in enumerate(dir_refs):
        incl = mask_ref[d, _MASK_INCL] > 0
        strict = mask_ref[d, _MASK_STRICT] > 0
        tri = mask_ref[d, _MASK_INCL][:, :c].astype(_BF16)
        for b in range(rows):
            lw = lw_ref[b]
            ci = _dot_exact_rhs(tri, lw, _split3)
            ce = ci - lw
            tot = ci[c - 1:c, :] if d == 0 else ci[0:1, :]
            g_inv = jnp.exp(-ci)
            g_tail = jnp.exp(tot - ci)
            kd = kd_ref[b]
            be = be_ref[b]
            kt_all = kap_ref[b] * jnp.exp(ce)
            rt_all = r_ref[b] * jnp.exp(ci)
            kh_all = kd * g_inv
            bh_all = be * g_inv
            kb_all = kd * g_tail
            bb_all = be * g_tail
            v_all = v_ref[b]
            g_col = jnp.broadcast_to(jnp.exp(tot), (LANE, tot.shape[1])).T
            for g in range(groups):
                s = slice(g * gw, (g + 1) * gw)
                chains.append(dict(
                    incl=incl, strict=strict, kt=kt_all[:, s], rt=rt_all[:, s], kh=kh_all[:, s],
                    bh=bh_all[:, s], kb=kb_all[:, s], bb=bb_all[:, s], v=v_all[:, s],
                    decay=jnp.concatenate([g_col[s, :]] * (gw // LANE), axis=1),
                    y_ref=y_ref, b=b, sl=s, st=(d * rows + b) * groups + g))

    n = range(len(chains))
    ch = chains
    states = [st_ref[ch[i]["st"]] for i in n]
    kr = [jnp.concatenate([ch[i]["kt"], ch[i]["rt"]], axis=0) for i in n]
    a_k = [_bdot(kr[i], bd(ch[i]["kh"]), _NT) for i in n]
    a_b = [_bdot(kr[i], bd(ch[i]["bh"]), _NT) for i in n]
    a_kk = [jnp.where(ch[i]["strict"], a_k[i][:c], 0.0) for i in n]
    a_rk = [jnp.where(ch[i]["incl"], a_k[i][c:], 0.0) for i in n]
    lmat = [jnp.where(ch[i]["strict"], a_b[i][:c], 0.0) for i in n]
    a_rb = [jnp.where(ch[i]["incl"], a_b[i][c:], 0.0) for i in n]
    pv = [_bdot(jnp.concatenate([a_kk[i], a_rk[i]], axis=0), bd(ch[i]["v"])) for i in n]
    p = [pv[i][:c] for i in n]
    y_kv = [pv[i][c:] for i in n]

    minv = [-jnp.where(blk2, lmat[i], 0.0) for i in n]
    for off in offs:
        loff = [jnp.where(off, lmat[i], 0.0) for i in n]
        q = [loff[i] + _bdot(minv[i], bd(loff[i])) for i in n]
        minv = [minv[i] - q[i] - _bdot(q[i], bd(minv[i])) for i in n]

    wt = [ch[i]["kt"] + _bdot(minv[i], bd(ch[i]["kt"])) for i in n]
    ut = [p[i] + _bdot(minv[i], bd(p[i])) for i in n]
    ws_rs = [_bdot(jnp.concatenate([wt[i], ch[i]["rt"]], axis=0), states[i]) for i in n]
    u = [ws_rs[i][:c] + ut[i] for i in n]
    y_u = [_bdot(a_rb[i], bd(u[i])) for i in n]
    dst = [_bdot(jnp.concatenate([ch[i]["kb"], -ch[i]["bb"]], axis=0),
                 jnp.concatenate([ch[i]["v"], u[i]], axis=0), _TN) for i in n]

    for i in n:
        ch[i]["y_ref"][ch[i]["b"], :, ch[i]["sl"]] = ws_rs[i][c:] + y_kv[i] - y_u[i]
        st_ref[ch[i]["st"]] = ch[i]["decay"] * states[i] + jnp.where(bd_keep, dst[i], 0.0)


def _scan(masks, bdmask, r, v, kap, kd, be, lw, *, batch, seq, width, rows):
    nc = seq // CHUNK
    gw = PACK * HEAD_DIM
    groups = width // gw
    to3 = lambda a: a.reshape(batch, seq, width)
    to4 = lambda a: a.reshape(2, batch, seq, width)
    fwd = pl.BlockSpec((rows, CHUNK, width), lambda i, c: (i, c, 0))
    bwd = pl.BlockSpec((rows, CHUNK, width), lambda i, c: (i, nc - 1 - c, 0))
    fwd_d = pl.BlockSpec((None, rows, CHUNK, width), lambda i, c: (0, i, c, 0))
    bwd_d = pl.BlockSpec((None, rows, CHUNK, width), lambda i, c: (1, i, nc - 1 - c, 0))
    r3, v3, kap3, kd4, be4, lw4 = to3(r), to3(v), to3(kap), to4(kd), to4(be), to4(lw)
    sds = jax.ShapeDtypeStruct((batch, seq, width), _F32)
    yf, yb = pl.pallas_call(
        functools.partial(_scan_kernel, groups=groups, rows=rows),
        grid=(batch // rows, nc),
        in_specs=[pl.BlockSpec(masks.shape, lambda i, c: (0, 0, 0, 0)),
                  pl.BlockSpec((gw, gw), lambda i, c: (0, 0)),
                  fwd, fwd, fwd, fwd_d, fwd_d, fwd_d, bwd, bwd, bwd, bwd_d, bwd_d, bwd_d],
        out_specs=[fwd, bwd],
        out_shape=[sds, sds],
        scratch_shapes=[pltpu.VMEM((2 * rows * groups, gw, gw), _F32)],
        compiler_params=_cparams(("arbitrary", "arbitrary")),
        name="rwkv_scan",
    )(masks, bdmask, r3, v3, kap3, kd4, be4, lw4, r3, v3, kap3, kd4, be4, lw4)
    return yf.reshape(batch * seq, width), yb.reshape(batch * seq, width)


def _band_bias(group):
    kj, qi = np.indices((3 * BLOCK, BLOCK))
    band = np.where(np.abs(kj - BLOCK - qi) <= WINDOW, 0.0, -np.inf).astype(np.float32)
    return np.tile(band, (1, group))


def _attn_kernel(sink_ref, bias_ref, q_ref, kp_ref, kc_ref, kn_ref, vp_ref, vc_ref, vn_ref,
                 cp_ref, cc_ref, cn_ref, sp_ref, sc_ref, sn_ref, o_ref, *, nb, group, kv_heads):
    n = pl.program_id(1)
    blk = BLOCK
    scale = HEAD_DIM ** -0.5 * LOG2E
    q = _rope(q_ref[...], cc_ref[...], sc_ref[...]) * scale
    k3 = jnp.concatenate([_rope(kp_ref[...], cp_ref[...], sp_ref[...]),
                          _rope(kc_ref[...], cc_ref[...], sc_ref[...]),
                          _rope(kn_ref[...], cn_ref[...], sn_ref[...])], axis=0)
    v3 = jnp.concatenate([vp_ref[...], vc_ref[...], vn_ref[...]], axis=0)

    no_prev = jnp.where(n == 0, -jnp.inf, 0.0)
    no_next = jnp.where(n == nb - 1, -jnp.inf, 0.0)
    bias = jnp.concatenate([bias_ref[0:blk, :] + no_prev, bias_ref[blk:2 * blk, :],
                            bias_ref[2 * blk:, :] + no_next], axis=0)
    head_of_lane = lax.broadcasted_iota(jnp.int32, (1, group * blk), 1) // blk

    gs = range(kv_heads)
    s_t, sk, o_t = [], [], []
    for g in gs:
        qs = jnp.concatenate(
            [q[:, (g * group + j) * HEAD_DIM:(g * group + j + 1) * HEAD_DIM] for j in range(group)], axis=0)
        s_t.append(_bdot(k3[:, g * HEAD_DIM:(g + 1) * HEAD_DIM], qs, _NT) + bias)
        row = jnp.zeros((1, group * blk), _F32)
        for j in range(group):
            row = jnp.where(head_of_lane == j, sink_ref[g * group + j] * LOG2E, row)
        sk.append(row)
    for g in gs:
        mx = jnp.maximum(jnp.max(s_t[g], axis=0, keepdims=True), sk[g])
        p_t = jnp.exp2(s_t[g] - mx)
        denom = jnp.sum(p_t, axis=0, keepdims=True) + jnp.exp2(sk[g] - mx)
        o_t.append(_bdot(v3[:, g * HEAD_DIM:(g + 1) * HEAD_DIM], p_t, _TN) / denom)
    for t in range(kv_heads * group // 2):
        g, j = divmod(2 * t, group)
        pair = jnp.concatenate([o_t[g][:, j * blk:(j + 1) * blk], o_t[g][:, (j + 1) * blk:(j + 2) * blk]], axis=0)
        o_ref[:, t * LANE:(t + 1) * LANE] = pair.T


def _attention(proj, cos_t, sin_t, sink_l, *, batch, seq, at_width, kv_width, q_block, k_block, v_block):
    m = proj.shape[0]
    nb = seq // BLOCK
    kv_heads = kv_width // HEAD_DIM
    group = at_width // kv_width
    assert group % 2 == 0 and BLOCK == LANE
    bias = jnp.asarray(_band_bias(group))
    cur = lambda b, n: b * nb + n
    prv = lambda b, n: b * nb + jnp.maximum(n - 1, 0)
    nxt = lambda b, n: b * nb + jnp.minimum(n + 1, nb - 1)

    def spec(width, rowfn, colblock):
        return pl.BlockSpec((BLOCK, width), lambda b, n: (rowfn(b, n), colblock))

    in_specs = [pl.BlockSpec(memory_space=pltpu.SMEM),
                pl.BlockSpec(bias.shape, lambda b, n: (0, 0)), spec(at_width, cur, q_block)]
    in_specs += [spec(kv_width, f, k_block) for f in (prv, cur, nxt)]
    in_specs += [spec(kv_width, f, v_block) for f in (prv, cur, nxt)]
    in_specs += [spec(LANE, f, 0) for f in (prv, cur, nxt)] * 2
    return pl.pallas_call(
        functools.partial(_attn_kernel, nb=nb, group=group, kv_heads=kv_heads),
        grid=(batch, nb),
        in_specs=in_specs,
        out_specs=pl.BlockSpec((BLOCK, at_width), lambda b, n: (cur(b, n), 0)),
        out_shape=jax.ShapeDtypeStruct((m, at_width), _F32),
        compiler_params=_cparams(("arbitrary", "arbitrary")),
        name="window_attention",
    )(sink_l, bias, proj, proj, proj, proj, proj, proj, proj, cos_t, cos_t, cos_t, sin_t, sin_t, sin_t)


def _silu(g):
    return g * jax.nn.sigmoid(g)


def _outproj_kernel(yf_ref, yb_ref, bv_ref, grw_ref, yat_ref, gat_ref, x_ref, lg_ref, lb_ref, g_ref,
                    wrw_ref, wat_ref, fg_ref, o_ref, mix_ref, *, width, final):
    gmat = g_ref[...]
    inv_n = 1.0 / HEAD_DIM

    def group_mean(z):
        return _bdot(z, gmat) * inv_n

    for s in range(width // SLAB):
        lo, hi = s * SLAB, (s + 1) * SLAB
        y = yf_ref[:, lo:hi] + yb_ref[:, lo:hi]
        yc = y - group_mean(y)
        var = group_mean(yc * yc)
        yn = yc * lax.rsqrt(var + LNX_EPS) * lg_ref[:, lo:hi] + lb_ref[:, lo:hi]
        mix_ref[:, lo:hi] = ((yn + bv_ref[:, lo:hi]) * _silu(grw_ref[:, lo:hi])).astype(_BF16)
    mix_at = (yat_ref[...] * _silu(gat_ref[...])).astype(_BF16)
    out = x_ref[...] + (jnp.dot(mix_ref[...], wrw_ref[...], preferred_element_type=_F32)
                        + jnp.dot(mix_at, wat_ref[...], preferred_element_type=_F32))
    if final:
        out = out * lax.rsqrt(jnp.mean(out * out, axis=-1, keepdims=True) + NORM_EPS) * fg_ref[...]
    o_ref[...] = out


def _outproj(yf, yb, bv, proj, yat, x2, lnx_g, lnx_b, gmat, w_rw, w_at, final_g, *, width, at_width,
             grw_block, gat_block, final, tm):
    m, d = x2.shape
    full = lambda shape: pl.BlockSpec(shape, lambda i: (0,) * len(shape))
    in_specs = [
        pl.BlockSpec((tm, width), lambda i: (i, 0)),
        pl.BlockSpec((tm, width), lambda i: (i, 0)),
        pl.BlockSpec((tm, width), lambda i: (i, 0)),
        pl.BlockSpec((tm, width), lambda i: (i, grw_block)),
        pl.BlockSpec((tm, at_width), lambda i: (i, 0)),
        pl.BlockSpec((tm, at_width), lambda i: (i, gat_block)),
        pl.BlockSpec((tm, d), lambda i: (i, 0)),
        full((1, width)), full((1, width)), full((SLAB, SLAB)),
        full((width, d)), full((at_width, d)), full((1, d)),
    ]
    return pl.pallas_call(
        functools.partial(_outproj_kernel, width=width, final=final),
        grid=(m // tm,),
        in_specs=in_specs,
        out_specs=pl.BlockSpec((tm, d), lambda i: (i, 0)),
        out_shape=jax.ShapeDtypeStruct((m, d), _F32),
        scratch_shapes=[pltpu.VMEM((tm, width), _BF16)],
        compiler_params=_cparams(("arbitrary",)),
        name="gate_outproj",
    )(yf, yb, bv, proj, yat, proj, x2, lnx_g, lnx_b, gmat, w_rw, w_at, final_g)


def _tile_plan(m, batch):
    return dict(inproj_tm=min(m, 1024), inproj_tn=3 * MXU_DIM, prep_tb=min(m, 256), out_tm=min(m, 256),
                scan_rows=2 if batch % 2 == 0 else 1)


def kernel(x, positions, norm_g, w_in, shift_mu, w0, decay_up, a0, iclr_up, k_k, k_a, r_k, lnx_g, lnx_b,
           sink, w_out, final_g):
    batch, seq, d_model = x.shape
    depth = w_in.shape[0]
    width = k_k.shape[-1]
    at_width = sink.shape[-1] * HEAD_DIM
    rank = decay_up.shape[2]
    in_cols = w_in.shape[-1]
    kv_width = (in_cols - (3 * width + 2 * rank) - width - 2 * at_width) // 2
    m = batch * seq
    assert iclr_up.shape[2] == rank and rank <= RANK_PAD
    assert width % SLAB == 0 and at_width == width and kv_width % LANE == 0
    assert seq % BLOCK == 0 and seq % CHUNK == 0 and BLOCK == WINDOW
    plan = _tile_plan(m, batch)
    assert m % plan["inproj_tm"] == 0 and seq % plan["prep_tb"] == 0 and m % plan["out_tm"] == 0

    o_xw = 3 * width
    o_xa = o_xw + rank
    o_grw = o_xa + rank
    o_q = o_grw + width
    o_k = o_q + at_width
    o_v = o_k + kv_width
    o_gat = o_v + kv_width
    pad_cols = lambda a: jnp.pad(a, [(0, 0)] * (a.ndim - 1) + [(0, RANK_PAD - rank)])
    w_perm = jnp.concatenate([
        w_in[..., :o_xw], w_in[..., o_grw:o_q], w_in[..., o_q:o_k], w_in[..., o_gat:],
        w_in[..., o_k:o_v], w_in[..., o_v:o_gat],
        pad_cols(w_in[..., o_xw:o_xa]), pad_cols(w_in[..., o_xa:o_grw])], axis=-1).astype(_BF16)
    n_cols = w_perm.shape[-1]
    assert n_cols % plan["inproj_tn"] == 0
    grw_block = (3 * width) // width
    q_block = grw_block + 1
    gat_block = q_block + 1
    kcol = 3 * width + width + 2 * at_width
    assert kcol % kv_width == 0 and (kcol + 2 * kv_width) % (2 * RANK_PAD) == 0
    k_block = kcol // kv_width
    v_block = k_block + 1
    xcol_block = (kcol + 2 * kv_width) // (2 * RANK_PAD)

    mu_rkv = shift_mu[..., :o_xw]
    mu_x = jnp.concatenate([pad_cols(shift_mu[..., o_xw:o_xa]), pad_cols(shift_mu[..., o_xa:o_grw])], axis=-1)
    pad_rows = lambda a: jnp.pad(a, [(0, 0), (0, 0), (0, RANK_PAD - rank), (0, 0)])
    dup = pad_rows(decay_up)
    iup = pad_rows(iclr_up)
    rk_flat = r_k.reshape(depth, 1, width)
    w_out_bf = w_out.astype(_BF16)

    lane_head = np.arange(SLAB) // HEAD_DIM
    gmat = jnp.asarray(lane_head[:, None] == lane_head[None, :], _BF16)
    masks = jnp.asarray(_scan_masks())

    x2 = x.reshape(m, d_model)
    cos_t, sin_t = _rope_tables(positions, m)
    for l in range(depth):
        proj = _inproj(x2, norm_g[l], w_perm[l], plan["inproj_tm"], plan["inproj_tn"])
        r, v, kap, kd, be, lw, bv = _prep(
            proj, mu_rkv[l], mu_x[l], w0[l], dup[l], a0[l], iup[l], k_k[l].reshape(1, width),
            k_a[l].reshape(1, width), rk_flat[l], gmat, seq=seq, width=width, xcol_block=xcol_block,
            tb=plan["prep_tb"])
        yf, yb = _scan(masks, gmat, r, v, kap, kd, be, lw, batch=batch, seq=seq, width=width,
                       rows=plan["scan_rows"])
        yat = _attention(proj, cos_t, sin_t, sink[l], batch=batch, seq=seq, at_width=at_width,
                         kv_width=kv_width, q_block=q_block, k_block=k_block, v_block=v_block)
        x2 = _outproj(yf, yb, bv, proj, yat, x2, lnx_g[l].reshape(1, width), lnx_b[l].reshape(1, width), gmat,
                      w_out_bf[l, :width], w_out_bf[l, width:], final_g.reshape(1, d_model), width=width,
                      at_width=at_width, grw_block=grw_block, gat_block=gat_block,
                      final=(l == depth - 1), tm=plan["out_tm"])
    return x2.reshape(batch, seq, d_model)
```

```python
import functools
import math

import numpy as np
import jax
import jax.numpy as jnp
from jax import lax
from jax.experimental import pallas as pl
from jax.experimental.pallas import tpu as pltpu

HEAD_DIM = 64
WINDOW = 128
BLOCK = 128
ROPE_THETA = 10000.0
NORM_EPS = 1e-6
LNX_EPS = 64e-5
KK_EPS = 1e-24
DECAY_BIAS = 0.5
LOG2E = math.log2(math.e)
EXP_NEG_DECAY_BIAS = math.exp(-DECAY_BIAS)

LANE = 128
SUBLANE = 8
HALO = 2 * SUBLANE
MXU_DIM = 256
VMEM_LIMIT_BYTES = 56 * 1024 * 1024

CHUNK = 64
SLAB = MXU_DIM
RANK_PAD = LANE

_F32 = jnp.float32
_BF16 = jnp.bfloat16
_NN = (((1,), (0,)), ((), ()))
_NT = (((1,), (1,)), ((), ()))
_TN = (((0,), (0,)), ((), ()))


def _bdot(a, b, dims=_NN):
    return lax.dot_general(a.astype(_BF16), b.astype(_BF16), dims, preferred_element_type=_F32)


def _split2(x):
    hi = x.astype(_BF16)
    lo = (x - hi.astype(_F32)).astype(_BF16)
    return hi, lo


def _split3(x):
    hi = x.astype(_BF16)
    r1 = x - hi.astype(_F32)
    mid = r1.astype(_BF16)
    lo = (r1 - mid.astype(_F32)).astype(_BF16)
    return hi, mid, lo


def _dot_exact_rhs(a_bf16, x, parts):
    out = None
    for p in parts(x):
        t = _bdot(a_bf16, p)
        out = t if out is None else out + t
    return out


def _cparams(semantics):
    return pltpu.CompilerParams(dimension_semantics=semantics, vmem_limit_bytes=VMEM_LIMIT_BYTES)


def _rope_table_kernel(pos_ref, inv_ref, cos_ref, sin_ref):
    ang = pos_ref[...].astype(_F32) * inv_ref[...]
    lane = lax.broadcasted_iota(jnp.int32, ang.shape, 1)
    first_half = (lane & (HEAD_DIM - 1)) < (HEAD_DIM // 2)
    s = jnp.sin(ang)
    cos_ref[...] = jnp.cos(ang)
    sin_ref[...] = jnp.where(first_half, -s, s)


def _rope_tables(positions, m):
    half = HEAD_DIM // 2
    inv = jnp.power(ROPE_THETA, -jnp.arange(half, dtype=_F32) / half)
    inv_row = jnp.tile(inv, LANE // half).reshape(1, LANE)
    tb = min(m, 2048)
    return pl.pallas_call(
        _rope_table_kernel,
        grid=(m // tb,),
        in_specs=[pl.BlockSpec((tb, 1), lambda i: (i, 0)), pl.BlockSpec((1, LANE), lambda i: (0, 0))],
        out_specs=[pl.BlockSpec((tb, LANE), lambda i: (i, 0))] * 2,
        out_shape=[jax.ShapeDtypeStruct((m, LANE), _F32)] * 2,
        compiler_params=_cparams(("arbitrary",)),
        name="rope_tables",
    )(positions.reshape(m, 1), inv_row)


def _rope(t, cos, sin):
    w = t.shape[1]
    half = HEAD_DIM // 2
    lane = lax.broadcasted_iota(jnp.int32, t.shape, 1)
    first_half = (lane & (HEAD_DIM - 1)) < half
    partner = jnp.where(first_half, pltpu.roll(t, w - half, 1), pltpu.roll(t, half, 1))
    reps = w // LANE
    return t * jnp.tile(cos, (1, reps)) + partner * jnp.tile(sin, (1, reps))


def _inproj_kernel(x_ref, g_ref, w_ref, o_ref, h_ref, *, tn):
    rows = min(x_ref.shape[0], 256)
    for r0 in range(0, x_ref.shape[0], rows):
        x = x_ref[r0:r0 + rows, :]
        y = x * lax.rsqrt(jnp.mean(x * x, axis=-1, keepdims=True) + NORM_EPS)
        h_ref[r0:r0 + rows, :] = (y * g_ref[...]).astype(_BF16)

    def col_tile(j, carry):
        cols = pl.ds(pl.multiple_of(j * tn, tn), tn)
        o_ref[:, cols] = jnp.dot(h_ref[...], w_ref[:, cols], preferred_element_type=_F32).astype(o_ref.dtype)
        return carry

    lax.fori_loop(0, w_ref.shape[1] // tn, col_tile, 0)


def _inproj(x2, g, w, tm, tn):
    m, d = x2.shape
    n = w.shape[1]
    return pl.pallas_call(
        functools.partial(_inproj_kernel, tn=tn),
        grid=(m // tm,),
        in_specs=[
            pl.BlockSpec((tm, d), lambda i: (i, 0)),
            pl.BlockSpec((1, d), lambda i: (0, 0)),
            pl.BlockSpec((d, n), lambda i: (0, 0), pipeline_mode=pl.Buffered(1)),
        ],
        out_specs=pl.BlockSpec((tm, n), lambda i: (i, 0)),
        out_shape=jax.ShapeDtypeStruct((m, n), _BF16),
        scratch_shapes=[pltpu.VMEM((tm, d), _BF16)],
        compiler_params=_cparams(("arbitrary",)),
        name="norm_inproj",
    )(x2, g.reshape(1, d), w)


def _prep_kernel(rkv_ref, rkvp_ref, rkvn_ref, xw_ref, xwp_ref, xwn_ref, murkv_ref, mux_ref,
                 w0_ref, dup_ref, a0_ref, iup_ref, kk_ref, ka_ref, rk_ref, g_ref,
                 r_out, v_out, kap_out, kd_out, be_out, lw_out, bv_out, *, tb, seq, width):
    i = pl.program_id(0)
    t0 = lax.rem(i * tb, seq)
    first = t0 == 0
    last = t0 + tb == seq
    row = lax.broadcasted_iota(jnp.int32, (tb, 1), 0)
    is_first_row = row == 0
    is_last_row = row == tb - 1
    gmat = g_ref[...]

    def shifted(main, prv, nxt, mu, lo, hi):
        u = main[:, lo:hi].astype(_F32)
        p_row = jnp.where(first, 0.0, prv[HALO - 1:HALO, lo:hi].astype(_F32))
        n_row = jnp.where(last, 0.0, nxt[0:1, lo:hi].astype(_F32))
        up = jnp.where(is_first_row, p_row, pltpu.roll(u, 1, 0))
        un = jnp.where(is_last_row, n_row, pltpu.roll(u, tb - 1, 0))
        return u + mu[0:1, lo:hi] * (up - u) + mu[1:2, lo:hi] * (un - u)

    def group_sum(z):
        return _bdot(z, gmat)

    xs = shifted(xw_ref, xwp_ref, xwn_ref, mux_ref, 0, 2 * RANK_PAD)
    lw_hi, lw_lo = _split2(jnp.tanh(xs[:, :RANK_PAD]))
    xa = xs[:, RANK_PAD:].astype(_BF16)

    for s in range(width // SLAB):
        lo, hi = s * SLAB, (s + 1) * SLAB
        r = shifted(rkv_ref, rkvp_ref, rkvn_ref, murkv_ref, lo, hi)
        k = shifted(rkv_ref, rkvp_ref, rkvn_ref, murkv_ref, width + lo, width + hi)
        v = shifted(rkv_ref, rkvp_ref, rkvn_ref, murkv_ref, 2 * width + lo, 2 * width + hi)
        kkv = k * kk_ref[:, lo:hi]
        ss = group_sum(kkv * kkv)
        kap = kkv * lax.rsqrt(jnp.maximum(ss, KK_EPS))
        kd_sum = None
        for d in range(2):
            dup_hi, dup_lo = _split2(dup_ref[d, :, lo:hi])
            wlin = w0_ref[d:d + 1, lo:hi] + (_bdot(lw_hi, dup_hi) + (_bdot(lw_hi, dup_lo) + _bdot(lw_lo, dup_hi)))
            lw_out[d, :, lo:hi] = -EXP_NEG_DECAY_BIAS * jax.nn.sigmoid(wlin)
            a = jax.nn.sigmoid(a0_ref[d:d + 1, lo:hi] + _bdot(xa, iup_ref[d, :, lo:hi]))
            kd = k * (1.0 + (a - 1.0) * ka_ref[:, lo:hi])
            kd_out[d, :, lo:hi] = kd.astype(kd_out.dtype)
            be_out[d, :, lo:hi] = (a * kap).astype(be_out.dtype)
            kd_sum = kd if kd_sum is None else kd_sum + kd
        bonus = group_sum(r * kd_sum * rk_ref[:, lo:hi])
        r_out[:, lo:hi] = r.astype(r_out.dtype)
        v_out[:, lo:hi] = v.astype(v_out.dtype)
        kap_out[:, lo:hi] = kap.astype(kap_out.dtype)
        bv_out[:, lo:hi] = (bonus * v).astype(bv_out.dtype)


def _prep(proj, mu_rkv, mu_x, w0, dup, a0, iup, k_k, k_a, r_k, gmat, *, seq, width, xcol_block, tb):
    m = proj.shape[0]
    nsub = tb // HALO
    last_sub = m // HALO - 1
    w3 = 3 * width
    full = lambda shape: pl.BlockSpec(shape, lambda i: (0,) * len(shape))
    prev_idx = lambda i: jnp.maximum(i * nsub - 1, 0)
    next_idx = lambda i: jnp.minimum((i + 1) * nsub, last_sub)
    in_specs = [
        pl.BlockSpec((tb, w3), lambda i: (i, 0)),
        pl.BlockSpec((HALO, w3), lambda i: (prev_idx(i), 0)),
        pl.BlockSpec((HALO, w3), lambda i: (next_idx(i), 0)),
        pl.BlockSpec((tb, 2 * RANK_PAD), lambda i: (i, xcol_block)),
        pl.BlockSpec((HALO, 2 * RANK_PAD), lambda i: (prev_idx(i), xcol_block)),
        pl.BlockSpec((HALO, 2 * RANK_PAD), lambda i: (next_idx(i), xcol_block)),
        full((2, w3)), full((2, 2 * RANK_PAD)),
        full((2, width)), full((2, RANK_PAD, width)), full((2, width)), full((2, RANK_PAD, width)),
        full((1, width)), full((1, width)), full((1, width)), full((SLAB, SLAB)),
    ]
    tok = pl.BlockSpec((tb, width), lambda i: (i, 0))
    tok2 = pl.BlockSpec((2, tb, width), lambda i: (0, i, 0))
    sds = jax.ShapeDtypeStruct((m, width), _BF16)
    sds2 = jax.ShapeDtypeStruct((2, m, width), _BF16)
    lw_sds = jax.ShapeDtypeStruct((2, m, width), _F32)
    return pl.pallas_call(
        functools.partial(_prep_kernel, tb=tb, seq=seq, width=width),
        grid=(m // tb,),
        in_specs=in_specs,
        out_specs=[tok, tok, tok, tok2, tok2, tok2, tok],
        out_shape=[sds, sds, sds, sds2, sds2, lw_sds, sds],
        compiler_params=_cparams(("arbitrary",)),
        name="rwkv_prep",
    )(proj, proj, proj, proj, proj, proj, mu_rkv, mu_x, w0, dup, a0, iup, k_k, k_a, r_k, gmat)


_N_LEVELS = int(np.log2(CHUNK)) - 1
_MASK_INCL, _MASK_STRICT, _MASK_BLK2, _MASK_OFF0 = 0, 1, 2, 3
PACK = MXU_DIM // HEAD_DIM
assert CHUNK == HEAD_DIM and LANE == 2 * HEAD_DIM


def _scan_masks():
    ri, cj = np.indices((CHUNK, CHUNK))
    out = np.zeros((2, 3 + _N_LEVELS, CHUNK, CHUNK), np.float32)
    for d in range(2):
        diff = (ri - cj) * (1 - 2 * d)
        out[d, _MASK_INCL] = diff >= 0
        out[d, _MASK_STRICT] = diff > 0
        out[d, _MASK_BLK2] = (ri >> 1) == (cj >> 1)
        for lv in range(_N_LEVELS):
            k = lv + 1
            out[d, _MASK_OFF0 + lv] = ((ri >> (k + 1)) == (cj >> (k + 1))) & ((ri >> k) != (cj >> k))
    return np.tile(out, (1, 1, 1, PACK))


def _scan_kernel(mask_ref, bd_ref, rf_ref, vf_ref, kapf_ref, kdf_ref, bef_ref, lwf_ref,
                 rb_ref, vb_ref, kapb_ref, kdb_ref, beb_ref, lwb_ref, yf_ref, yb_ref, st_ref, *, groups, rows):
    c = CHUNK
    gw = PACK * HEAD_DIM

    @pl.when(pl.program_id(1) == 0)
    def _():
        st_ref[...] = jnp.zeros_like(st_ref)

    blk2 = mask_ref[0, _MASK_BLK2] > 0
    offs = [mask_ref[0, _MASK_OFF0 + lv] > 0 for lv in range(_N_LEVELS)]
    bdm = bd_ref[...]
    bd_keep = bdm > 0

    keep_lo = bdm[0:c, 0:LANE]
    keep_hi = bdm[HEAD_DIM:HEAD_DIM + c, 0:LANE]
    zero = jnp.zeros((c, LANE), _BF16)

    def bd(x):
        xb = x.astype(_BF16)
        blocks = []
        for t in range(gw // LANE):
            col = xb[:, t * LANE:(t + 1) * LANE]
            for keep in (keep_lo, keep_hi):
                row = [zero] * (gw // LANE)
                row[t] = col * keep
                blocks.append(jnp.concatenate(row, axis=1))
        return jnp.concatenate(blocks, axis=0)

    dir_refs = ((rf_ref, vf_ref, kapf_ref, kdf_ref, bef_ref, lwf_ref, yf_ref),
                (rb_ref, vb_ref, kapb_ref, kdb_ref, beb_ref, lwb_ref, yb_ref))
    chains = []
    for d, (r_ref, v_ref, kap_ref, kd_ref, be_ref, lw_ref, y_ref) in enumerate(dir_refs):
        incl = mask_ref[d, _MASK_INCL] > 0
        strict = mask_ref[d, _MASK_STRICT] > 0
        tri = mask_ref[d, _MASK_INCL][:, :c].astype(_BF16)
        for b in range(rows):
            lw = lw_ref[b]
            ci = _dot_exact_rhs(tri, lw, _split3)
            ce = ci - lw
            tot = ci[c - 1:c, :] if d == 0 else ci[0:1, :]
            g_inv = jnp.exp(-ci)
            g_tail = jnp.exp(tot - ci)
            kd = kd_ref[b].astype(_F32)
            be = be_ref[b].astype(_F32)
            kt_all = kap_ref[b].astype(_F32) * jnp.exp(ce)
            rt_all = r_ref[b].astype(_F32) * jnp.exp(ci)
            kh_all = kd * g_inv
            bh_all = be * g_inv
            kb_all = kd * g_tail
            bb_all = be * g_tail
            v_all = v_ref[b]
            g_col = jnp.broadcast_to(jnp.exp(tot), (LANE, tot.shape[1])).T
            for g in range(groups):
                s = slice(g * gw, (g + 1) * gw)
                chains.append(dict(
                    incl=incl, strict=strict, kt=kt_all[:, s], rt=rt_all[:, s], kh=kh_all[:, s],
                    bh=bh_all[:, s], kb=kb_all[:, s], bb=bb_all[:, s], v=v_all[:, s],
                    decay=jnp.concatenate([g_col[s, :]] * (gw // LANE), axis=1),
                    y_ref=y_ref, b=b, sl=s, st=(d * rows + b) * groups + g))

    n = range(len(chains))
    ch = chains
    states = [st_ref[ch[i]["st"]] for i in n]
    kr = [jnp.concatenate([ch[i]["kt"], ch[i]["rt"]], axis=0) for i in n]
    a_k = [_bdot(kr[i], bd(ch[i]["kh"]), _NT) for i in n]
    a_b = [_bdot(kr[i], bd(ch[i]["bh"]), _NT) for i in n]
    a_kk = [jnp.where(ch[i]["strict"], a_k[i][:c], 0.0) for i in n]
    a_rk = [jnp.where(ch[i]["incl"], a_k[i][c:], 0.0) for i in n]
    lmat = [jnp.where(ch[i]["strict"], a_b[i][:c], 0.0) for i in n]
    a_rb = [jnp.where(ch[i]["incl"], a_b[i][c:], 0.0) for i in n]
    pv = [_bdot(jnp.concatenate([a_kk[i], a_rk[i]], axis=0), bd(ch[i]["v"])) for i in n]
    p = [pv[i][:c] for i in n]
    y_kv = [pv[i][c:] for i in n]

    minv = [-jnp.where(blk2, lmat[i], 0.0) for i in n]
    for off in offs:
        loff = [jnp.where(off, lmat[i], 0.0) for i in n]
        q = [loff[i] + _bdot(minv[i], bd(loff[i])) for i in n]
        minv = [minv[i] - q[i] - _bdot(q[i], bd(minv[i])) for i in n]

    wt = [ch[i]["kt"] + _bdot(minv[i], bd(ch[i]["kt"])) for i in n]
    ut = [p[i] + _bdot(minv[i], bd(p[i])) for i in n]
    ws_rs = [_bdot(jnp.concatenate([wt[i], ch[i]["rt"]], axis=0), states[i]) for i in n]
    u = [ws_rs[i][:c] + ut[i] for i in n]
    y_u = [_bdot(a_rb[i], bd(u[i])) for i in n]
    dst = [_bdot(jnp.concatenate([ch[i]["kb"], -ch[i]["bb"]], axis=0),
                 jnp.concatenate([ch[i]["v"].astype(_BF16), u[i].astype(_BF16)], axis=0), _TN) for i in n]

    for i in n:
        ch[i]["y_ref"][ch[i]["b"], :, ch[i]["sl"]] = (ws_rs[i][c:] + y_kv[i] - y_u[i]).astype(yf_ref.dtype)
        st_ref[ch[i]["st"]] = ch[i]["decay"] * states[i] + jnp.where(bd_keep, dst[i], 0.0)


def _scan(masks, bdmask, r, v, kap, kd, be, lw, *, batch, seq, width, rows):
    nc = seq // CHUNK
    gw = PACK * HEAD_DIM
    groups = width // gw
    to3 = lambda a: a.reshape(batch, seq, width)
    to4 = lambda a: a.reshape(2, batch, seq, width)
    fwd = pl.BlockSpec((rows, CHUNK, width), lambda i, c: (i, c, 0))
    bwd = pl.BlockSpec((rows, CHUNK, width), lambda i, c: (i, nc - 1 - c, 0))
    fwd_d = pl.BlockSpec((None, rows, CHUNK, width), lambda i, c: (0, i, c, 0))
    bwd_d = pl.BlockSpec((None, rows, CHUNK, width), lambda i, c: (1, i, nc - 1 - c, 0))
    r3, v3, kap3, kd4, be4, lw4 = to3(r), to3(v), to3(kap), to4(kd), to4(be), to4(lw)
    sds = jax.ShapeDtypeStruct((batch, seq, width), _BF16)
    yf, yb = pl.pallas_call(
        functools.partial(_scan_kernel, groups=groups, rows=rows),
        grid=(batch // rows, nc),
        in_specs=[pl.BlockSpec(masks.shape, lambda i, c: (0, 0, 0, 0)),
                  pl.BlockSpec((gw, gw), lambda i, c: (0, 0)),
                  fwd, fwd, fwd, fwd_d, fwd_d, fwd_d, bwd, bwd, bwd, bwd_d, bwd_d, bwd_d],
        out_specs=[fwd, bwd],
        out_shape=[sds, sds],
        scratch_shapes=[pltpu.VMEM((2 * rows * groups, gw, gw), _F32)],
        compiler_params=_cparams(("arbitrary", "arbitrary")),
        name="rwkv_scan",
    )(masks, bdmask, r3, v3, kap3, kd4, be4, lw4, r3, v3, kap3, kd4, be4, lw4)
    return yf.reshape(batch * seq, width), yb.reshape(batch * seq, width)


def _band_bias(group):
    kj, qi = np.indices((3 * BLOCK, BLOCK))
    band = np.where(np.abs(kj - BLOCK - qi) <= WINDOW, 0.0, -np.inf).astype(np.float32)
    return np.tile(band, (1, group))


def _attn_kernel(sink_ref, bias_ref, q_ref, kp_ref, kc_ref, kn_ref, vp_ref, vc_ref, vn_ref,
                 cp_ref, cc_ref, cn_ref, sp_ref, sc_ref, sn_ref, o_ref, *, nb, group, kv_heads):
    n = pl.program_id(1)
    blk = BLOCK
    scale = HEAD_DIM ** -0.5 * LOG2E
    q = _rope(q_ref[...].astype(_F32), cc_ref[...], sc_ref[...]) * scale
    k3 = jnp.concatenate([_rope(kp_ref[...].astype(_F32), cp_ref[...], sp_ref[...]),
                          _rope(kc_ref[...].astype(_F32), cc_ref[...], sc_ref[...]),
                          _rope(kn_ref[...].astype(_F32), cn_ref[...], sn_ref[...])], axis=0)
    v3 = jnp.concatenate([vp_ref[...], vc_ref[...], vn_ref[...]], axis=0)

    no_prev = jnp.where(n == 0, -jnp.inf, 0.0)
    no_next = jnp.where(n == nb - 1, -jnp.inf, 0.0)
    bias = jnp.concatenate([bias_ref[0:blk, :] + no_prev, bias_ref[blk:2 * blk, :],
                            bias_ref[2 * blk:, :] + no_next], axis=0)
    head_of_lane = lax.broadcasted_iota(jnp.int32, (1, group * blk), 1) // blk

    gs = range(kv_heads)
    s_t, sk, o_t = [], [], []
    for g in gs:
        qs = jnp.concatenate(
            [q[:, (g * group + j) * HEAD_DIM:(g * group + j + 1) * HEAD_DIM] for j in range(group)], axis=0)
        s_t.append(_bdot(k3[:, g * HEAD_DIM:(g + 1) * HEAD_DIM], qs, _NT) + bias)
        row = jnp.zeros((1, group * blk), _F32)
        for j in range(group):
            row = jnp.where(head_of_lane == j, sink_ref[g * group + j] * LOG2E, row)
        sk.append(row)
    for g in gs:
        mx = jnp.maximum(jnp.max(s_t[g], axis=0, keepdims=True), sk[g])
        p_t = jnp.exp2(s_t[g] - mx)
        denom = jnp.sum(p_t, axis=0, keepdims=True) + jnp.exp2(sk[g] - mx)
        o_t.append(_bdot(v3[:, g * HEAD_DIM:(g + 1) * HEAD_DIM], p_t, _TN) / denom)
    for t in range(kv_heads * group // 2):
        g, j = divmod(2 * t, group)
        pair = jnp.concatenate([o_t[g][:, j * blk:(j + 1) * blk], o_t[g][:, (j + 1) * blk:(j + 2) * blk]], axis=0)
        o_ref[:, t * LANE:(t + 1) * LANE] = pair.T.astype(o_ref.dtype)


def _attention(proj, cos_t, sin_t, sink_l, *, batch, seq, at_width, kv_width, q_block, k_block, v_block):
    m = proj.shape[0]
    nb = seq // BLOCK
    kv_heads = kv_width // HEAD_DIM
    group = at_width // kv_width
    assert group % 2 == 0 and BLOCK == LANE
    bias = jnp.asarray(_band_bias(group))
    cur = lambda b, n: b * nb + n
    prv = lambda b, n: b * nb + jnp.maximum(n - 1, 0)
    nxt = lambda b, n: b * nb + jnp.minimum(n + 1, nb - 1)

    def spec(width, rowfn, colblock):
        return pl.BlockSpec((BLOCK, width), lambda b, n: (rowfn(b, n), colblock))

    in_specs = [pl.BlockSpec(memory_space=pltpu.SMEM),
                pl.BlockSpec(bias.shape, lambda b, n: (0, 0)), spec(at_width, cur, q_block)]
    in_specs += [spec(kv_width, f, k_block) for f in (prv, cur, nxt)]
    in_specs += [spec(kv_width, f, v_block) for f in (prv, cur, nxt)]
    in_specs += [spec(LANE, f, 0) for f in (prv, cur, nxt)] * 2
    return pl.pallas_call(
        functools.partial(_attn_kernel, nb=nb, group=group, kv_heads=kv_heads),
        grid=(batch, nb),
        in_specs=in_specs,
        out_specs=pl.BlockSpec((BLOCK, at_width), lambda b, n: (cur(b, n), 0)),
        out_shape=jax.ShapeDtypeStruct((m, at_width), _BF16),
        compiler_params=_cparams(("arbitrary", "arbitrary")),
        name="window_attention",
    )(sink_l, bias, proj, proj, proj, proj, proj, proj, proj, cos_t, cos_t, cos_t, sin_t, sin_t, sin_t)


def _silu(g):
    return g * jax.nn.sigmoid(g)


def _outproj_kernel(yf_ref, yb_ref, bv_ref, grw_ref, yat_ref, gat_ref, x_ref, lg_ref, lb_ref, g_ref,
                    wrw_ref, wat_ref, fg_ref, o_ref, mix_ref, *, width, final):
    gmat = g_ref[...]
    inv_n = 1.0 / HEAD_DIM

    def group_mean(z):
        return _bdot(z, gmat) * inv_n

    for s in range(width // SLAB):
        lo, hi = s * SLAB, (s + 1) * SLAB
        y = yf_ref[:, lo:hi].astype(_F32) + yb_ref[:, lo:hi].astype(_F32)
        yc = y - group_mean(y)
        var = group_mean(yc * yc)
        yn = yc * lax.rsqrt(var + LNX_EPS) * lg_ref[:, lo:hi] + lb_ref[:, lo:hi]
        gate = _silu(grw_ref[:, lo:hi].astype(_F32))
        mix_ref[:, lo:hi] = ((yn + bv_ref[:, lo:hi].astype(_F32)) * gate).astype(_BF16)
    mix_at = (yat_ref[...].astype(_F32) * _silu(gat_ref[...].astype(_F32))).astype(_BF16)
    out = x_ref[...] + (jnp.dot(mix_ref[...], wrw_ref[...], preferred_element_type=_F32)
                        + jnp.dot(mix_at, wat_ref[...], preferred_element_type=_F32))
    if final:
        out = out * lax.rsqrt(jnp.mean(out * out, axis=-1, keepdims=True) + NORM_EPS) * fg_ref[...]
    o_ref[...] = out


def _outproj(yf, yb, bv, proj, yat, x2, lnx_g, lnx_b, gmat, w_rw, w_at, final_g, *, width, at_width,
             grw_block, gat_block, final, tm):
    m, d = x2.shape
    full = lambda shape: pl.BlockSpec(shape, lambda i: (0,) * len(shape))
    in_specs = [
        pl.BlockSpec((tm, width), lambda i: (i, 0)),
        pl.BlockSpec((tm, width), lambda i: (i, 0)),
        pl.BlockSpec((tm, width), lambda i: (i, 0)),
        pl.BlockSpec((tm, width), lambda i: (i, grw_block)),
        pl.BlockSpec((tm, at_width), lambda i: (i, 0)),
        pl.BlockSpec((tm, at_width), lambda i: (i, gat_block)),
        pl.BlockSpec((tm, d), lambda i: (i, 0)),
        full((1, width)), full((1, width)), full((SLAB, SLAB)),
        full((width, d)), full((at_width, d)), full((1, d)),
    ]
    return pl.pallas_call(
        functools.partial(_outproj_kernel, width=width, final=final),
        grid=(m // tm,),
        in_specs=in_specs,
        out_specs=pl.BlockSpec((tm, d), lambda i: (i, 0)),
        out_shape=jax.ShapeDtypeStruct((m, d), _F32),
        scratch_shapes=[pltpu.VMEM((tm, width), _BF16)],
        compiler_params=_cparams(("arbitrary",)),
        name="gate_outproj",
    )(yf, yb, bv, proj, yat, proj, x2, lnx_g, lnx_b, gmat, w_rw, w_at, final_g)


def _tile_plan(m, batch):
    return dict(inproj_tm=min(m, 512), inproj_tn=3 * MXU_DIM, prep_tb=min(m, 256), out_tm=min(m, 256),
                scan_rows=2 if batch % 2 == 0 else 1)


def kernel(x, positions, norm_g, w_in, shift_mu, w0, decay_up, a0, iclr_up, k_k, k_a, r_k, lnx_g, lnx_b,
           sink, w_out, final_g):
    batch, seq, d_model = x.shape
    depth = w_in.shape[0]
    width = k_k.shape[-1]
    at_width = sink.shape[-1] * HEAD_DIM
    rank = decay_up.shape[2]
    in_cols = w_in.shape[-1]
    kv_width = (in_cols - (3 * width + 2 * rank) - width - 2 * at_width) // 2
    m = batch * seq
    assert iclr_up.shape[2] == rank and rank <= RANK_PAD
    assert width % SLAB == 0 and at_width == width and kv_width % LANE == 0
    assert seq % BLOCK == 0 and seq % CHUNK == 0 and BLOCK == WINDOW
    plan = _tile_plan(m, batch)
    assert m % plan["inproj_tm"] == 0 and seq % plan["prep_tb"] == 0 and m % plan["out_tm"] == 0

    o_xw = 3 * width
    o_xa = o_xw + rank
    o_grw = o_xa + rank
    o_q = o_grw + width
    o_k = o_q + at_width
    o_v = o_k + kv_width
    o_gat = o_v + kv_width
    pad_cols = lambda a: jnp.pad(a, [(0, 0)] * (a.ndim - 1) + [(0, RANK_PAD - rank)])
    w_perm = jnp.concatenate([
        w_in[..., :o_xw], w_in[..., o_grw:o_q], w_in[..., o_q:o_k], w_in[..., o_gat:],
        w_in[..., o_k:o_v], w_in[..., o_v:o_gat],
        pad_cols(w_in[..., o_xw:o_xa]), pad_cols(w_in[..., o_xa:o_grw])], axis=-1).astype(_BF16)
    n_cols = w_perm.shape[-1]
    assert n_cols % plan["inproj_tn"] == 0
    grw_block = (3 * width) // width
    q_block = grw_block + 1
    gat_block = q_block + 1
    kcol = 3 * width + width + 2 * at_width
    assert kcol % kv_width == 0 and (kcol + 2 * kv_width) % (2 * RANK_PAD) == 0
    k_block = kcol // kv_width
    v_block = k_block + 1
    xcol_block = (kcol + 2 * kv_width) // (2 * RANK_PAD)

    mu_rkv = shift_mu[..., :o_xw]
    mu_x = jnp.concatenate([pad_cols(shift_mu[..., o_xw:o_xa]), pad_cols(shift_mu[..., o_xa:o_grw])], axis=-1)
    pad_rows = lambda a: jnp.pad(a, [(0, 0), (0, 0), (0, RANK_PAD - rank), (0, 0)])
    dup = pad_rows(decay_up)
    iup = pad_rows(iclr_up)
    rk_flat = r_k.reshape(depth, 1, width)
    w_out_bf = w_out.astype(_BF16)

    lane_head = np.arange(SLAB) // HEAD_DIM
    gmat = jnp.asarray(lane_head[:, None] == lane_head[None, :], _BF16)
    masks = jnp.asarray(_scan_masks())

    x2 = x.reshape(m, d_model)
    cos_t, sin_t = _rope_tables(positions, m)
    for l in range(depth):
        proj = _inproj(x2, norm_g[l], w_perm[l], plan["inproj_tm"], plan["inproj_tn"])
        r, v, kap, kd, be, lw, bv = _prep(
            proj, mu_rkv[l], mu_x[l], w0[l], dup[l], a0[l], iup[l], k_k[l].reshape(1, width),
            k_a[l].reshape(1, width), rk_flat[l], gmat, seq=seq, width=width, xcol_block=xcol_block,
            tb=plan["prep_tb"])
        yf, yb = _scan(masks, gmat, r, v, kap, kd, be, lw, batch=batch, seq=seq, width=width,
                       rows=plan["scan_rows"])
        yat = _attention(proj, cos_t, sin_t, sink[l], batch=batch, seq=seq, at_width=at_width,
                         kv_width=kv_width, q_block=q_block, k_block=k_block, v_block=v_block)
        x2 = _outproj(yf, yb, bv, proj, yat, x2, lnx_g[l].reshape(1, width), lnx_b[l].reshape(1, width), gmat,
                      w_out_bf[l, :width], w_out_bf[l, width:], final_g.reshape(1, d_model), width=width,
                      at_width=at_width, grw_block=grw_block, gat_block=gat_block,
                      final=(l == depth - 1), tm=plan["out_tm"])
    return x2.reshape(batch, seq, d_model)
```

```python
import functools
import math

import numpy as np
import jax
import jax.numpy as jnp
from jax import lax
from jax.experimental import pallas as pl
from jax.experimental.pallas import tpu as pltpu

HEAD_DIM = 64
WINDOW = 128
BLOCK = 128
ROPE_THETA = 10000.0
NORM_EPS = 1e-6
LNX_EPS = 64e-5
KK_EPS = 1e-24
DECAY_BIAS = 0.5
LOG2E = math.log2(math.e)
EXP_NEG_DECAY_BIAS = math.exp(-DECAY_BIAS)

LANE = 128
SUBLANE = 8
HALO = 2 * SUBLANE
MXU_DIM = 256
VMEM_LIMIT_BYTES = 56 * 1024 * 1024

CHUNK = 64
SLAB = MXU_DIM
RANK_PAD = LANE

_F32 = jnp.float32
_BF16 = jnp.bfloat16
_NN = (((1,), (0,)), ((), ()))
_NT = (((1,), (1,)), ((), ()))
_TN = (((0,), (0,)), ((), ()))


def _bdot(a, b, dims=_NN):
    return lax.dot_general(a.astype(_BF16), b.astype(_BF16), dims, preferred_element_type=_F32)


def _split2(x):
    hi = x.astype(_BF16)
    lo = (x - hi.astype(_F32)).astype(_BF16)
    return hi, lo


def _split3(x):
    hi = x.astype(_BF16)
    r1 = x - hi.astype(_F32)
    mid = r1.astype(_BF16)
    lo = (r1 - mid.astype(_F32)).astype(_BF16)
    return hi, mid, lo


def _dot_exact_rhs(a_bf16, x, parts):
    out = None
    for p in parts(x):
        t = _bdot(a_bf16, p)
        out = t if out is None else out + t
    return out


def _cparams(semantics):
    return pltpu.CompilerParams(dimension_semantics=semantics, vmem_limit_bytes=VMEM_LIMIT_BYTES)


def _rope_table_kernel(pos_ref, inv_ref, cos_ref, sin_ref):
    ang = pos_ref[...].astype(_F32) * inv_ref[...]
    lane = lax.broadcasted_iota(jnp.int32, ang.shape, 1)
    first_half = (lane & (HEAD_DIM - 1)) < (HEAD_DIM // 2)
    s = jnp.sin(ang)
    cos_ref[...] = jnp.cos(ang)
    sin_ref[...] = jnp.where(first_half, -s, s)


def _rope_tables(positions, m):
    half = HEAD_DIM // 2
    inv = jnp.power(ROPE_THETA, -jnp.arange(half, dtype=_F32) / half)
    inv_row = jnp.tile(inv, LANE // half).reshape(1, LANE)
    tb = min(m, 2048)
    return pl.pallas_call(
        _rope_table_kernel,
        grid=(m // tb,),
        in_specs=[pl.BlockSpec((tb, 1), lambda i: (i, 0)), pl.BlockSpec((1, LANE), lambda i: (0, 0))],
        out_specs=[pl.BlockSpec((tb, LANE), lambda i: (i, 0))] * 2,
        out_shape=[jax.ShapeDtypeStruct((m, LANE), _F32)] * 2,
        compiler_params=_cparams(("arbitrary",)),
        name="rope_tables",
    )(positions.reshape(m, 1), inv_row)


def _rope(t, cos, sin):
    w = t.shape[1]
    half = HEAD_DIM // 2
    lane = lax.broadcasted_iota(jnp.int32, t.shape, 1)
    first_half = (lane & (HEAD_DIM - 1)) < half
    partner = jnp.where(first_half, pltpu.roll(t, w - half, 1), pltpu.roll(t, half, 1))
    reps = w // LANE
    return t * jnp.tile(cos, (1, reps)) + partner * jnp.tile(sin, (1, reps))


def _relayout_kernel(w_ref, o_ref, *, sections):
    covered = 0
    for lo, hi, dst in sections:
        if dst > covered:
            o_ref[:, covered:dst] = jnp.zeros((o_ref.shape[0], dst - covered), o_ref.dtype)
        o_ref[:, dst:dst + hi - lo] = w_ref[:, lo:hi].astype(o_ref.dtype)
        covered = dst + hi - lo
    if covered < o_ref.shape[1]:
        o_ref[:, covered:] = jnp.zeros((o_ref.shape[0], o_ref.shape[1] - covered), o_ref.dtype)


def _relayout_weight(w, sections, n_cols):
    depth, rows, cols = w.shape
    tr = min(rows, 256)
    return pl.pallas_call(
        functools.partial(_relayout_kernel, sections=sections),
        grid=(depth, rows // tr),
        in_specs=[pl.BlockSpec((None, tr, cols), lambda l, i: (l, i, 0))],
        out_specs=pl.BlockSpec((None, tr, n_cols), lambda l, i: (l, i, 0)),
        out_shape=jax.ShapeDtypeStruct((depth, rows, n_cols), _BF16),
        compiler_params=_cparams(("arbitrary", "arbitrary")),
        name="weight_layout",
    )(w)


def _inproj_kernel(x_ref, g_ref, w_ref, o_ref, h_ref, *, tn):
    rows = min(x_ref.shape[0], 256)
    for r0 in range(0, x_ref.shape[0], rows):
        x = x_ref[r0:r0 + rows, :]
        y = x * lax.rsqrt(jnp.mean(x * x, axis=-1, keepdims=True) + NORM_EPS)
        h_ref[r0:r0 + rows, :] = (y * g_ref[...]).astype(_BF16)

    def col_tile(j, carry):
        cols = pl.ds(pl.multiple_of(j * tn, tn), tn)
        o_ref[:, cols] = jnp.dot(h_ref[...], w_ref[:, cols], preferred_element_type=_F32).astype(o_ref.dtype)
        return carry

    lax.fori_loop(0, w_ref.shape[1] // tn, col_tile, 0)


def _inproj(x2, g, w, layer, tm, tn):
    m, d = x2.shape
    n = w.shape[2]
    return pl.pallas_call(
        functools.partial(_inproj_kernel, tn=tn),
        grid=(m // tm,),
        in_specs=[
            pl.BlockSpec((tm, d), lambda i: (i, 0)),
            pl.BlockSpec((1, d), lambda i: (0, 0)),
            pl.BlockSpec((None, d, n), lambda i: (layer, 0, 0), pipeline_mode=pl.Buffered(1)),
        ],
        out_specs=pl.BlockSpec((tm, n), lambda i: (i, 0)),
        out_shape=jax.ShapeDtypeStruct((m, n), _BF16),
        scratch_shapes=[pltpu.VMEM((tm, d), _BF16)],
        compiler_params=_cparams(("arbitrary",)),
        name="norm_inproj",
    )(x2, g.reshape(1, d), w)


def _prep_kernel(rkv_ref, rkvp_ref, rkvn_ref, xw_ref, xwp_ref, xwn_ref, murkv_ref, mux_ref,
                 w0_ref, dup_ref, a0_ref, iup_ref, kk_ref, ka_ref, rk_ref, g_ref,
                 r_out, v_out, kap_out, kd_out, be_out, lw_out, bv_out, *, tb, seq, width):
    i = pl.program_id(0)
    t0 = lax.rem(i * tb, seq)
    first = t0 == 0
    last = t0 + tb == seq
    row = lax.broadcasted_iota(jnp.int32, (tb, 1), 0)
    is_first_row = row == 0
    is_last_row = row == tb - 1
    gmat = g_ref[...]

    def shifted(main, prv, nxt, mu, lo, hi):
        u = main[:, lo:hi].astype(_F32)
        p_row = jnp.where(first, 0.0, prv[HALO - 1:HALO, lo:hi].astype(_F32))
        n_row = jnp.where(last, 0.0, nxt[0:1, lo:hi].astype(_F32))
        up = jnp.where(is_first_row, p_row, pltpu.roll(u, 1, 0))
        un = jnp.where(is_last_row, n_row, pltpu.roll(u, tb - 1, 0))
        return u + mu[0:1, lo:hi] * (up - u) + mu[1:2, lo:hi] * (un - u)

    def group_sum(z):
        return _bdot(z, gmat)

    xs = shifted(xw_ref, xwp_ref, xwn_ref, mux_ref, 0, 2 * RANK_PAD)
    lw_hi, lw_lo = _split2(jnp.tanh(xs[:, :RANK_PAD]))
    xa = xs[:, RANK_PAD:].astype(_BF16)

    for s in range(width // SLAB):
        lo, hi = s * SLAB, (s + 1) * SLAB
        r = shifted(rkv_ref, rkvp_ref, rkvn_ref, murkv_ref, lo, hi)
        k = shifted(rkv_ref, rkvp_ref, rkvn_ref, murkv_ref, width + lo, width + hi)
        v = shifted(rkv_ref, rkvp_ref, rkvn_ref, murkv_ref, 2 * width + lo, 2 * width + hi)
        kkv = k * kk_ref[:, lo:hi]
        ss = group_sum(kkv * kkv)
        kap = kkv * lax.rsqrt(jnp.maximum(ss, KK_EPS))
        kd_sum = None
        for d in range(2):
            dup_hi, dup_lo = _split2(dup_ref[d, :, lo:hi])
            wlin = w0_ref[d:d + 1, lo:hi] + (_bdot(lw_hi, dup_hi) + (_bdot(lw_hi, dup_lo) + _bdot(lw_lo, dup_hi)))
            lw_out[d, :, lo:hi] = -EXP_NEG_DECAY_BIAS * jax.nn.sigmoid(wlin)
            a = jax.nn.sigmoid(a0_ref[d:d + 1, lo:hi] + _bdot(xa, iup_ref[d, :, lo:hi]))
            kd = k * (1.0 + (a - 1.0) * ka_ref[:, lo:hi])
            kd_out[d, :, lo:hi] = kd.astype(kd_out.dtype)
            be_out[d, :, lo:hi] = (a * kap).astype(be_out.dtype)
            kd_sum = kd if kd_sum is None else kd_sum + kd
        bonus = group_sum(r * kd_sum * rk_ref[:, lo:hi])
        r_out[:, lo:hi] = r.astype(r_out.dtype)
        v_out[:, lo:hi] = v.astype(v_out.dtype)
        kap_out[:, lo:hi] = kap.astype(kap_out.dtype)
        bv_out[:, lo:hi] = (bonus * v).astype(bv_out.dtype)


def _prep(proj, mu_rkv, mu_x, w0, dup, a0, iup, k_k, k_a, r_k, gmat, *, seq, width, xcol_block, tb):
    m = proj.shape[0]
    nsub = tb // HALO
    last_sub = m // HALO - 1
    w3 = 3 * width
    full = lambda shape: pl.BlockSpec(shape, lambda i: (0,) * len(shape))
    prev_idx = lambda i: jnp.maximum(i * nsub - 1, 0)
    next_idx = lambda i: jnp.minimum((i + 1) * nsub, last_sub)
    in_specs = [
        pl.BlockSpec((tb, w3), lambda i: (i, 0)),
        pl.BlockSpec((HALO, w3), lambda i: (prev_idx(i), 0)),
        pl.BlockSpec((HALO, w3), lambda i: (next_idx(i), 0)),
        pl.BlockSpec((tb, 2 * RANK_PAD), lambda i: (i, xcol_block)),
        pl.BlockSpec((HALO, 2 * RANK_PAD), lambda i: (prev_idx(i), xcol_block)),
        pl.BlockSpec((HALO, 2 * RANK_PAD), lambda i: (next_idx(i), xcol_block)),
        full((2, w3)), full((2, 2 * RANK_PAD)),
        full((2, width)), full((2, RANK_PAD, width)), full((2, width)), full((2, RANK_PAD, width)),
        full((1, width)), full((1, width)), full((1, width)), full((SLAB, SLAB)),
    ]
    tok = pl.BlockSpec((tb, width), lambda i: (i, 0))
    tok2 = pl.BlockSpec((2, tb, width), lambda i: (0, i, 0))
    sds = jax.ShapeDtypeStruct((m, width), _BF16)
    sds2 = jax.ShapeDtypeStruct((2, m, width), _BF16)
    lw_sds = jax.ShapeDtypeStruct((2, m, width), _F32)
    return pl.pallas_call(
        functools.partial(_prep_kernel, tb=tb, seq=seq, width=width),
        grid=(m // tb,),
        in_specs=in_specs,
        out_specs=[tok, tok, tok, tok2, tok2, tok2, tok],
        out_shape=[sds, sds, sds, sds2, sds2, lw_sds, sds],
        compiler_params=_cparams(("arbitrary",)),
        name="rwkv_prep",
    )(proj, proj, proj, proj, proj, proj, mu_rkv, mu_x, w0, dup, a0, iup, k_k, k_a, r_k, gmat)


_N_LEVELS = int(np.log2(CHUNK)) - 1
_MASK_INCL, _MASK_STRICT, _MASK_BLK2, _MASK_OFF0 = 0, 1, 2, 3
PACK = MXU_DIM // HEAD_DIM
assert CHUNK == HEAD_DIM and LANE == 2 * HEAD_DIM


def _scan_masks():
    ri, cj = np.indices((CHUNK, CHUNK))
    out = np.zeros((2, 3 + _N_LEVELS, CHUNK, CHUNK), np.float32)
    for d in range(2):
        diff = (ri - cj) * (1 - 2 * d)
        out[d, _MASK_INCL] = diff >= 0
        out[d, _MASK_STRICT] = diff > 0
        out[d, _MASK_BLK2] = (ri >> 1) == (cj >> 1)
        for lv in range(_N_LEVELS):
            k = lv + 1
            out[d, _MASK_OFF0 + lv] = ((ri >> (k + 1)) == (cj >> (k + 1))) & ((ri >> k) != (cj >> k))
    return np.tile(out, (1, 1, 1, PACK))


def _scan_kernel(mask_ref, bd_ref, rf_ref, vf_ref, kapf_ref, kdf_ref, bef_ref, lwf_ref,
                 rb_ref, vb_ref, kapb_ref, kdb_ref, beb_ref, lwb_ref, yf_ref, yb_ref, st_ref, *, groups, rows):
    c = CHUNK
    gw = PACK * HEAD_DIM

    @pl.when(pl.program_id(1) == 0)
    def _():
        st_ref[...] = jnp.zeros_like(st_ref)

    blk2 = mask_ref[0, _MASK_BLK2] > 0
    offs = [mask_ref[0, _MASK_OFF0 + lv] > 0 for lv in range(_N_LEVELS)]
    bdm = bd_ref[...]
    bd_keep = bdm > 0

    keep_lo = bdm[0:c, 0:LANE]
    keep_hi = bdm[HEAD_DIM:HEAD_DIM + c, 0:LANE]
    zero = jnp.zeros((c, LANE), _BF16)

    def bd(x):
        xb = x.astype(_BF16)
        blocks = []
        for t in range(gw // LANE):
            col = xb[:, t * LANE:(t + 1) * LANE]
            for keep in (keep_lo, keep_hi):
                row = [zero] * (gw // LANE)
                row[t] = col * keep
                blocks.append(jnp.concatenate(row, axis=1))
        return jnp.concatenate(blocks, axis=0)

    dir_refs =((rf_ref, vf_ref, kapf_ref, kdf_ref, bef_ref, lwf_ref, yf_ref),
                (rb_ref, vb_ref, kapb_ref, kdb_ref, beb_ref, lwb_ref, yb_ref))
    chains = []
    for d, (r_ref, v_ref, kap_ref, kd_ref, be_ref, lw_ref, y_ref) in enumerate(dir_refs):
        incl = mask_ref[d, _MASK_INCL] > 0
        strict = mask_ref[d, _MASK_STRICT] > 0
        tri = mask_ref[d, _MASK_INCL][:, :c].astype(_BF16)
        for b in range(rows):
            lw = lw_ref[b]
            ci = _dot_exact_rhs(tri, lw, _split3)
            ce = ci - lw
            tot = ci[c - 1:c, :] if d == 0 else ci[0:1, :]
            g_inv = jnp.exp(-ci)
            g_tail = jnp.exp(tot - ci)
            kd = kd_ref[b].astype(_F32)
            be = be_ref[b].astype(_F32)
            kt_all = kap_ref[b].astype(_F32) * jnp.exp(ce)
            rt_all = r_ref[b].astype(_F32) * jnp.exp(ci)
            kh_all = kd * g_inv
            bh_all = be * g_inv
            kb_all = kd * g_tail
            bb_all = be * g_tail
            v_all = v_ref[b]
            g_col = jnp.broadcast_to(jnp.exp(tot), (LANE, tot.shape[1])).T
            for g in range(groups):
                s = slice(g * gw, (g + 1) * gw)
                chains.append(dict(
                    incl=incl, strict=strict, kt=kt_all[:, s], rt=rt_all[:, s], kh=kh_all[:, s],
                    bh=bh_all[:, s], kb=kb_all[:, s], bb=bb_all[:, s], v=v_all[:, s],
                    decay=jnp.concatenate([g_col[s, :]] * (gw // LANE), axis=1),
                    y_ref=y_ref, b=b, sl=s, st=(d * rows + b) * groups + g))

    n = range(len(chains))
    ch = chains
    states = [st_ref[ch[i]["st"]] for i in n]
    kr = [jnp.concatenate([ch[i]["kt"], ch[i]["rt"]], axis=0) for i in n]
    a_k = [_bdot(kr[i], bd(ch[i]["kh"]), _NT) for i in n]
    a_b = [_bdot(kr[i], bd(ch[i]["bh"]), _NT) for i in n]
    a_kk = [jnp.where(ch[i]["strict"], a_k[i][:c], 0.0) for i in n]
    a_rk = [jnp.where(ch[i]["incl"], a_k[i][c:], 0.0) for i in n]
    lmat = [jnp.where(ch[i]["strict"], a_b[i][:c], 0.0) for i in n]
    a_rb = [jnp.where(ch[i]["incl"], a_b[i][c:], 0.0) for i in n]
    pv = [_bdot(jnp.concatenate([a_kk[i], a_rk[i]], axis=0), bd(ch[i]["v"])) for i in n]
    p = [pv[i][:c] for i in n]
    y_kv = [pv[i][c:] for i in n]

    minv = [-jnp.where(blk2, lmat[i], 0.0) for i in n]
    for off in offs:
        loff = [jnp.where(off, lmat[i], 0.0) for i in n]
        q = [loff[i] + _bdot(minv[i], bd(loff[i])) for i in n]
        minv = [minv[i] - q[i] - _bdot(q[i], bd(minv[i])) for i in n]

    wt = [ch[i]["kt"] + _bdot(minv[i], bd(ch[i]["kt"])) for i in n]
    ut = [p[i] + _bdot(minv[i], bd(p[i])) for i in n]
    ws_rs = [_bdot(jnp.concatenate([wt[i], ch[i]["rt"]], axis=0), states[i]) for i in n]
    u = [ws_rs[i][:c] + ut[i] for i in n]
    y_u = [_bdot(a_rb[i], bd(u[i])) for i in n]
    dst = [_bdot(jnp.concatenate([ch[i]["kb"], -ch[i]["bb"]], axis=0),
                 jnp.concatenate([ch[i]["v"].astype(_BF16), u[i].astype(_BF16)], axis=0), _TN) for i in n]

    for i in n:
        ch[i]["y_ref"][ch[i]["b"], :, ch[i]["sl"]] = (ws_rs[i][c:] + y_kv[i] - y_u[i]).astype(yf_ref.dtype)
        st_ref[ch[i]["st"]] = ch[i]["decay"] * states[i] + jnp.where(bd_keep, dst[i], 0.0)


def _scan(masks, bdmask, r, v, kap, kd, be, lw, *, batch, seq, width, rows):
    nc = seq // CHUNK
    gw = PACK * HEAD_DIM
    groups = width // gw
    to3 = lambda a: a.reshape(batch, seq, width)
    to4 = lambda a: a.reshape(2, batch, seq, width)
    fwd = pl.BlockSpec((rows, CHUNK, width), lambda i, c: (i, c, 0))
    bwd = pl.BlockSpec((rows, CHUNK, width), lambda i, c: (i, nc - 1 - c, 0))
    fwd_d = pl.BlockSpec((None, rows, CHUNK, width), lambda i, c: (0, i, c, 0))
    bwd_d = pl.BlockSpec((None, rows, CHUNK, width), lambda i, c: (1, i, nc - 1 - c, 0))
    r3, v3, kap3, kd4, be4, lw4 = to3(r), to3(v), to3(kap), to4(kd), to4(be), to4(lw)
    sds = jax.ShapeDtypeStruct((batch, seq, width), _BF16)
    yf, yb = pl.pallas_call(
        functools.partial(_scan_kernel, groups=groups, rows=rows),
        grid=(batch // rows, nc),
        in_specs=[pl.BlockSpec(masks.shape, lambda i, c: (0, 0, 0, 0)),
                  pl.BlockSpec((gw, gw), lambda i, c: (0, 0)),
                  fwd, fwd, fwd, fwd_d, fwd_d, fwd_d, bwd, bwd, bwd, bwd_d, bwd_d, bwd_d],
        out_specs=[fwd, bwd],
        out_shape=[sds, sds],
        scratch_shapes=[pltpu.VMEM((2 * rows * groups, gw, gw), _F32)],
        compiler_params=_cparams(("arbitrary", "arbitrary")),
        name="rwkv_scan",
    )(masks, bdmask, r3, v3, kap3, kd4, be4, lw4, r3, v3, kap3, kd4, be4, lw4)
    return yf.reshape(batch * seq, width), yb.reshape(batch * seq, width)


def _band_bias(group):
    kj, qi = np.indices((3 * BLOCK, BLOCK))
    band = np.where(np.abs(kj - BLOCK - qi) <= WINDOW, 0.0, -np.inf).astype(np.float32)
    return np.tile(band, (1, group))


def _attn_kernel(sink_ref, bias_ref, q_ref, kp_ref, km_ref, kn_ref, vp_ref, vm_ref, vn_ref,
                 cp_ref, cm_ref, cn_ref, sp_ref, sm_ref, sn_ref, o_ref, *, nb, group, kv_heads):
    i = pl.program_id(1)
    blk = BLOCK
    scale = HEAD_DIM ** -0.5 * LOG2E
    q = _rope(q_ref[...].astype(_F32), cm_ref[...], sm_ref[...]) * scale
    k4 = jnp.concatenate([_rope(kp_ref[...].astype(_F32), cp_ref[...], sp_ref[...]),
                          _rope(km_ref[...].astype(_F32), cm_ref[...], sm_ref[...]),
                          _rope(kn_ref[...].astype(_F32), cn_ref[...], sn_ref[...])], axis=0)
    v4 = jnp.concatenate([vp_ref[...], vm_ref[...], vn_ref[...]], axis=0)

    no_prev = jnp.where(i == 0, -jnp.inf, 0.0)
    no_next = jnp.where(2 * i + 1 == nb - 1, -jnp.inf, 0.0)
    biases = (jnp.concatenate([bias_ref[0:blk, :] + no_prev, bias_ref[blk:, :]], axis=0),
              jnp.concatenate([bias_ref[0:2 * blk, :], bias_ref[2 * blk:, :] + no_next], axis=0))
    head_of_lane = lax.broadcasted_iota(jnp.int32, (1, group * blk), 1) // blk

    chains = [(a, g) for a in range(2) for g in range(kv_heads)]
    sk = []
    for g in range(kv_heads):
        row = jnp.zeros((1, group * blk), _F32)
        for j in range(group):
            row = jnp.where(head_of_lane == j, sink_ref[g * group + j] * LOG2E, row)
        sk.append(row)
    s_t, o_t = [], {}
    for a, g in chains:
        qs = jnp.concatenate([q[a * blk:(a + 1) * blk, (g * group + j) * HEAD_DIM:(g * group + j + 1) * HEAD_DIM]
                              for j in range(group)], axis=0)
        keys = k4[a * blk:(a + 3) * blk, g * HEAD_DIM:(g + 1) * HEAD_DIM]
        s_t.append(_bdot(keys, qs, _NT) + biases[a])
    for c, (a, g) in enumerate(chains):
        mx = jnp.maximum(jnp.max(s_t[c], axis=0, keepdims=True), sk[g])
        p_t = jnp.exp2(s_t[c] - mx)
        denom = jnp.sum(p_t, axis=0, keepdims=True) + jnp.exp2(sk[g] - mx)
        vals = v4[a * blk:(a + 3) * blk, g * HEAD_DIM:(g + 1) * HEAD_DIM]
        o_t[a, g] = _bdot(vals, p_t, _TN) / denom
    for a in range(2):
        for t in range(kv_heads * group // 2):
            g, j = divmod(2 * t, group)
            pair = jnp.concatenate([o_t[a, g][:, j * blk:(j + 1) * blk],
                                    o_t[a, g][:, (j + 1) * blk:(j + 2) * blk]], axis=0)
            o_ref[a * blk:(a + 1) * blk, t * LANE:(t + 1) * LANE] = pair.T.astype(o_ref.dtype)


def _attention(proj, cos_t, sin_t, sink_l, *, batch, seq, at_width, kv_width, q_block, k_block, v_block):
    m = proj.shape[0]
    nb = seq // BLOCK
    kv_heads = kv_width // HEAD_DIM
    group = at_width // kv_width
    assert group % 2 == 0 and BLOCK == LANE
    bias = jnp.asarray(_band_bias(group))
    assert nb % 2 == 0
    half = nb // 2
    mid = (2 * BLOCK, lambda b, i: b * half + i)
    prv = (BLOCK, lambda b, i: b * nb + jnp.maximum(2 * i - 1, 0))
    nxt = (BLOCK, lambda b, i: b * nb + jnp.minimum(2 * i + 2, nb - 1))

    def spec(width, where, colblock):
        rows, rowfn = where
        return pl.BlockSpec((rows, width), lambda b, i: (rowfn(b, i), colblock))

    in_specs = [pl.BlockSpec(memory_space=pltpu.SMEM),
                pl.BlockSpec(bias.shape, lambda b, i: (0, 0)), spec(at_width, mid, q_block)]
    in_specs += [spec(kv_width, f, k_block) for f in (prv, mid, nxt)]
    in_specs += [spec(kv_width, f, v_block) for f in (prv, mid, nxt)]
    in_specs += [spec(LANE, f, 0) for f in (prv, mid, nxt)] * 2
    return pl.pallas_call(
        functools.partial(_attn_kernel, nb=nb, group=group, kv_heads=kv_heads),
        grid=(batch, half),
        in_specs=in_specs,
        out_specs=spec(at_width, mid, 0),
        out_shape=jax.ShapeDtypeStruct((m, at_width), _BF16),
        compiler_params=_cparams(("arbitrary", "arbitrary")),
        name="window_attention",
    )(sink_l, bias, proj, proj, proj, proj, proj, proj, proj, cos_t, cos_t, cos_t, sin_t, sin_t, sin_t)


def _silu(g):
    return g * jax.nn.sigmoid(g)


def _outproj_kernel(yf_ref, yb_ref, bv_ref, grw_ref, yat_ref, gat_ref, x_ref, lg_ref, lb_ref, g_ref,
                    wrw_ref, wat_ref, fg_ref, o_ref, mix_ref, *, width, final):
    gmat = g_ref[...]
    inv_n = 1.0 / HEAD_DIM

    def group_mean(z):
        return _bdot(z, gmat) * inv_n

    for s in range(width // SLAB):
        lo, hi = s * SLAB, (s + 1) * SLAB
        y = yf_ref[:, lo:hi].astype(_F32) + yb_ref[:, lo:hi].astype(_F32)
        yc = y - group_mean(y)
        var = group_mean(yc * yc)
        yn = yc * lax.rsqrt(var + LNX_EPS) * lg_ref[:, lo:hi] + lb_ref[:, lo:hi]
        gate = _silu(grw_ref[:, lo:hi].astype(_F32))
        mix_ref[:, lo:hi] = ((yn + bv_ref[:, lo:hi].astype(_F32)) * gate).astype(_BF16)
    mix_at = (yat_ref[...].astype(_F32) * _silu(gat_ref[...].astype(_F32))).astype(_BF16)
    out = x_ref[...] + (jnp.dot(mix_ref[...], wrw_ref[...], preferred_element_type=_F32)
                        + jnp.dot(mix_at, wat_ref[...], preferred_element_type=_F32))
    if final:
        out = out * lax.rsqrt(jnp.mean(out * out, axis=-1, keepdims=True) + NORM_EPS) * fg_ref[...]
    o_ref[...] = out


def _outproj(yf, yb, bv, proj, yat, x2, lnx_g, lnx_b, gmat, w_out, layer, final_g, *, width, at_width,
             grw_block, gat_block, final, tm):
    m, d = x2.shape
    assert width == at_width
    full = lambda shape: pl.BlockSpec(shape, lambda i: (0,) * len(shape))
    in_specs = [
        pl.BlockSpec((tm, width), lambda i: (i, 0)),
        pl.BlockSpec((tm, width), lambda i: (i, 0)),
        pl.BlockSpec((tm, width), lambda i: (i, 0)),
        pl.BlockSpec((tm, width), lambda i: (i, grw_block)),
        pl.BlockSpec((tm, at_width), lambda i: (i, 0)),
        pl.BlockSpec((tm, at_width), lambda i: (i, gat_block)),
        pl.BlockSpec((tm, d), lambda i: (i, 0)),
        full((1, width)), full((1, width)), full((SLAB, SLAB)),
        pl.BlockSpec((None, width, d), lambda i: (layer, 0, 0)),
        pl.BlockSpec((None, at_width, d), lambda i: (layer, 1, 0)), full((1, d)),
    ]
    return pl.pallas_call(
        functools.partial(_outproj_kernel, width=width, final=final),
        grid=(m // tm,),
        in_specs=in_specs,
        out_specs=pl.BlockSpec((tm, d), lambda i: (i, 0)),
        out_shape=jax.ShapeDtypeStruct((m, d), _F32),
        scratch_shapes=[pltpu.VMEM((tm, width), _BF16)],
        compiler_params=_cparams(("arbitrary",)),
        name="gate_outproj",
    )(yf, yb, bv, proj, yat, proj, x2, lnx_g, lnx_b, gmat, w_out, w_out, final_g)


def _tile_plan(m, batch):
    return dict(inproj_tm=min(m, 512), inproj_tn=9 * MXU_DIM, prep_tb=min(m, 256), out_tm=min(m, 256),
                scan_rows=2 if batch % 2 == 0 else 1)


def kernel(x, positions, norm_g, w_in, shift_mu, w0, decay_up, a0, iclr_up, k_k, k_a, r_k, lnx_g, lnx_b,
           sink, w_out, final_g):
    batch, seq, d_model = x.shape
    depth = w_in.shape[0]
    width = k_k.shape[-1]
    at_width = sink.shape[-1] * HEAD_DIM
    rank = decay_up.shape[2]
    in_cols = w_in.shape[-1]
    kv_width = (in_cols - (3 * width + 2 * rank) - width - 2 * at_width) // 2
    m = batch * seq
    assert iclr_up.shape[2] == rank and rank <= RANK_PAD
    assert width % SLAB == 0 and at_width == width and kv_width % LANE == 0
    assert seq % BLOCK == 0 and seq % CHUNK == 0 and BLOCK == WINDOW
    plan = _tile_plan(m, batch)
    assert m % plan["inproj_tm"] == 0 and seq % plan["prep_tb"] == 0 and m % plan["out_tm"] == 0

    o_xw = 3 * width
    o_xa = o_xw + rank
    o_grw = o_xa + rank
    o_q = o_grw + width
    o_k = o_q + at_width
    o_v = o_k + kv_width
    o_gat = o_v + kv_width
    pad_cols = lambda a: jnp.pad(a, [(0, 0)] * (a.ndim - 1) + [(0, RANK_PAD - rank)])
    sources = [(0, o_xw), (o_grw, o_q), (o_q, o_k), (o_gat, in_cols), (o_k, o_v), (o_v, o_gat)]
    sections, dst = [], 0
    for lo, hi in sources:
        sections.append((lo, hi, dst))
        dst += hi - lo
    for lo, hi in ((o_xw, o_xa), (o_xa, o_grw)):
        sections.append((lo, hi, dst))
        dst += RANK_PAD
    n_cols = dst
    w_perm = _relayout_weight(w_in, tuple(sections), n_cols)
    assert n_cols % plan["inproj_tn"] == 0
    grw_block = (3 * width) // width
    q_block = grw_block + 1
    gat_block = q_block + 1
    kcol = 3 * width + width + 2 * at_width
    assert kcol % kv_width == 0 and (kcol + 2 * kv_width) % (2 * RANK_PAD) == 0
    k_block = kcol // kv_width
    v_block = k_block + 1
    xcol_block = (kcol + 2 * kv_width) // (2 * RANK_PAD)

    mu_rkv = shift_mu[..., :o_xw]
    mu_x = jnp.concatenate([pad_cols(shift_mu[..., o_xw:o_xa]), pad_cols(shift_mu[..., o_xa:o_grw])], axis=-1)
    pad_rows = lambda a: jnp.pad(a, [(0, 0), (0, 0), (0, RANK_PAD - rank), (0, 0)])
    dup = pad_rows(decay_up)
    iup = pad_rows(iclr_up)
    rk_flat = r_k.reshape(depth, 1, width)
    w_out_bf = w_out.astype(_BF16)

    lane_head = np.arange(SLAB) // HEAD_DIM
    gmat = jnp.asarray(lane_head[:, None] == lane_head[None, :], _BF16)
    masks = jnp.asarray(_scan_masks())

    x2 = x.reshape(m, d_model)
    cos_t, sin_t = _rope_tables(positions, m)
    for l in range(depth):
        proj = _inproj(x2, norm_g[l], w_perm, l, plan["inproj_tm"], plan["inproj_tn"])
        r, v, kap, kd, be, lw, bv = _prep(
            proj, mu_rkv[l], mu_x[l], w0[l], dup[l], a0[l], iup[l], k_k[l].reshape(1, width),
            k_a[l].reshape(1, width), rk_flat[l], gmat, seq=seq, width=width, xcol_block=xcol_block,
            tb=plan["prep_tb"])
        yf, yb = _scan(masks, gmat, r, v, kap, kd, be, lw, batch=batch, seq=seq, width=width,
                       rows=plan["scan_rows"])
        yat = _attention(proj, cos_t, sin_t, sink[l], batch=batch, seq=seq, at_width=at_width,
                         kv_width=kv_width, q_block=q_block, k_block=k_block, v_block=v_block)
        x2 = _outproj(yf, yb, bv, proj, yat, x2, lnx_g[l].reshape(1, width), lnx_b[l].reshape(1, width), gmat,
                      w_out_bf, l, final_g.reshape(1, d_model), width=width,
                      at_width=at_width, grw_block=grw_block, gat_block=gat_block,
                      final=(l == depth - 1), tm=plan["out_tm"])
    return x2.reshape(batch, seq, d_model)
```

```python
import functools
import math

import numpy as np
import jax
import jax.numpy as jnp
from jax import lax
from jax.experimental import pallas as pl
from jax.experimental.pallas import tpu as pltpu

HEAD_DIM = 64
WINDOW = 128
BLOCK = 128
ROPE_THETA = 10000.0
NORM_EPS = 1e-6
LNX_EPS = 64e-5
KK_EPS = 1e-24
DECAY_BIAS = 0.5
LOG2E = math.log2(math.e)
EXP_NEG_DECAY_BIAS = math.exp(-DECAY_BIAS)

LANE = 128
SUBLANE = 8
HALO = 2 * SUBLANE
MXU_DIM = 256
VMEM_LIMIT_BYTES = 56 * 1024 * 1024

CHUNK = 64
SLAB = MXU_DIM
RANK_PAD = LANE

_F32 = jnp.float32
_BF16 = jnp.bfloat16
_NN = (((1,), (0,)), ((), ()))
_NT = (((1,), (1,)), ((), ()))
_TN = (((0,), (0,)), ((), ()))


def _bdot(a, b, dims=_NN):
    return lax.dot_general(a.astype(_BF16), b.astype(_BF16), dims, preferred_element_type=_F32)


def _split2(x):
    hi = x.astype(_BF16)
    lo = (x - hi.astype(_F32)).astype(_BF16)
    return hi, lo


def _split3(x):
    hi = x.astype(_BF16)
    r1 = x - hi.astype(_F32)
    mid = r1.astype(_BF16)
    lo = (r1 - mid.astype(_F32)).astype(_BF16)
    return hi, mid, lo


def _dot_exact_rhs(a_bf16, x, parts):
    out = None
    for p in parts(x):
        t = _bdot(a_bf16, p)
        out = t if out is None else out + t
    return out


def _cparams(semantics):
    return pltpu.CompilerParams(dimension_semantics=semantics, vmem_limit_bytes=VMEM_LIMIT_BYTES)


def _rope_table_kernel(pos_ref, inv_ref, cos_ref, sin_ref):
    ang = pos_ref[...].astype(_F32) * inv_ref[...]
    lane = lax.broadcasted_iota(jnp.int32, ang.shape, 1)
    first_half = (lane & (HEAD_DIM - 1)) < (HEAD_DIM // 2)
    s = jnp.sin(ang)
    cos_ref[...] = jnp.cos(ang)
    sin_ref[...] = jnp.where(first_half, -s, s)


def _rope_tables(positions, m):
    half = HEAD_DIM // 2
    inv = jnp.power(ROPE_THETA, -jnp.arange(half, dtype=_F32) / half)
    inv_row = jnp.tile(inv, LANE // half).reshape(1, LANE)
    tb = min(m, 2048)
    return pl.pallas_call(
        _rope_table_kernel,
        grid=(m // tb,),
        in_specs=[pl.BlockSpec((tb, 1), lambda i: (i, 0)), pl.BlockSpec((1, LANE), lambda i: (0, 0))],
        out_specs=[pl.BlockSpec((tb, LANE), lambda i: (i, 0))] * 2,
        out_shape=[jax.ShapeDtypeStruct((m, LANE), _F32)] * 2,
        compiler_params=_cparams(("arbitrary",)),
        name="rope_tables",
    )(positions.reshape(m, 1), inv_row)


def _rope(t, cos, sin):
    w = t.shape[1]
    half = HEAD_DIM // 2
    lane = lax.broadcasted_iota(jnp.int32, t.shape, 1)
    first_half = (lane & (HEAD_DIM - 1)) < half
    partner = jnp.where(first_half, pltpu.roll(t, w - half, 1), pltpu.roll(t, half, 1))
    reps = w // LANE
    return t * jnp.tile(cos, (1, reps)) + partner * jnp.tile(sin, (1, reps))


def _relayout_kernel(w_ref, o_ref, *, sections):
    covered = 0
    for lo, hi, dst in sections:
        if dst > covered:
            o_ref[:, covered:dst] = jnp.zeros((o_ref.shape[0], dst - covered), o_ref.dtype)
        o_ref[:, dst:dst + hi - lo] = w_ref[:, lo:hi].astype(o_ref.dtype)
        covered = dst + hi - lo
    if covered < o_ref.shape[1]:
        o_ref[:, covered:] = jnp.zeros((o_ref.shape[0], o_ref.shape[1] - covered), o_ref.dtype)


def _relayout_weight(w, sections, n_cols):
    depth, rows, cols = w.shape
    tr = min(rows, 256)
    return pl.pallas_call(
        functools.partial(_relayout_kernel, sections=sections),
        grid=(depth, rows // tr),
        in_specs=[pl.BlockSpec((None, tr, cols), lambda l, i: (l, i, 0))],
        out_specs=pl.BlockSpec((None, tr, n_cols), lambda l, i: (l, i, 0)),
        out_shape=jax.ShapeDtypeStruct((depth, rows, n_cols), _BF16),
        compiler_params=_cparams(("arbitrary", "arbitrary")),
        name="weight_layout",
    )(w)


def _inproj_kernel(x_ref, g_ref, w_ref, o_ref, h_ref, *, tn):
    rows = min(x_ref.shape[0], 256)
    for r0 in range(0, x_ref.shape[0], rows):
        x = x_ref[r0:r0 + rows, :]
        y = x * lax.rsqrt(jnp.mean(x * x, axis=-1, keepdims=True) + NORM_EPS)
        h_ref[r0:r0 + rows, :] = (y * g_ref[...]).astype(_BF16)

    def col_tile(j, carry):
        cols = pl.ds(pl.multiple_of(j * tn, tn), tn)
        o_ref[:, cols] = jnp.dot(h_ref[...], w_ref[:, cols], preferred_element_type=_F32).astype(o_ref.dtype)
        return carry

    lax.fori_loop(0, w_ref.shape[1] // tn, col_tile, 0)


def _inproj(x2, g, w, layer, tm, tn):
    m, d = x2.shape
    n = w.shape[2]
    return pl.pallas_call(
        functools.partial(_inproj_kernel, tn=tn),
        grid=(m // tm,),
        in_specs=[
            pl.BlockSpec((tm, d), lambda i: (i, 0)),
            pl.BlockSpec((1, d), lambda i: (0, 0)),
            pl.BlockSpec((None, d, n), lambda i: (layer, 0, 0), pipeline_mode=pl.Buffered(1)),
        ],
        out_specs=pl.BlockSpec((tm, n), lambda i: (i, 0)),
        out_shape=jax.ShapeDtypeStruct((m, n), _BF16),
        scratch_shapes=[pltpu.VMEM((tm, d), _BF16)],
        compiler_params=_cparams(("arbitrary",)),
        name="norm_inproj",
    )(x2, g.reshape(1, d), w)


def _prep_kernel(rkv_ref, rkvp_ref, rkvn_ref, xw_ref, xwp_ref, xwn_ref, murkv_ref, mux_ref,
                 w0_ref, dup_ref, a0_ref, iup_ref, kk_ref, ka_ref, rk_ref, g_ref,
                 r_out, v_out, kap_out, kd_out, be_out, lw_out, bv_out, *, tb, seq, width):
    i = pl.program_id(0)
    t0 = lax.rem(i * tb, seq)
    first = t0 == 0
    last = t0 + tb == seq
    row = lax.broadcasted_iota(jnp.int32, (tb, 1), 0)
    is_first_row = row == 0
    is_last_row = row == tb - 1
    gmat = g_ref[...]

    def shifted(main, prv, nxt, mu, lo, hi):
        u = main[:, lo:hi].astype(_F32)
        p_row = jnp.where(first, 0.0, prv[HALO - 1:HALO, lo:hi].astype(_F32))
        n_row = jnp.where(last, 0.0, nxt[0:1, lo:hi].astype(_F32))
        up = jnp.where(is_first_row, p_row, pltpu.roll(u, 1, 0))
        un = jnp.where(is_last_row, n_row, pltpu.roll(u, tb - 1, 0))
        mu0, mu1 = mu[0:1, lo:hi], mu[1:2, lo:hi]
        return (1.0 - mu0 - mu1) * u + mu0 * up + mu1 * un

    def group_sum(z):
        return _bdot(z, gmat)

    xs = shifted(xw_ref, xwp_ref, xwn_ref, mux_ref, 0, 2 * RANK_PAD)
    lw_hi, lw_lo = _split2(jnp.tanh(xs[:, :RANK_PAD]))
    xa = xs[:, RANK_PAD:].astype(_BF16)

    for s in range(width // SLAB):
        lo, hi = s * SLAB, (s + 1) * SLAB
        r = shifted(rkv_ref, rkvp_ref, rkvn_ref, murkv_ref, lo, hi)
        k = shifted(rkv_ref, rkvp_ref, rkvn_ref, murkv_ref, width + lo, width + hi)
        v = shifted(rkv_ref, rkvp_ref, rkvn_ref, murkv_ref, 2 * width + lo, 2 * width + hi)
        kkv = k * kk_ref[:, lo:hi]
        ss = group_sum(kkv * kkv)
        kap = kkv * lax.rsqrt(jnp.maximum(ss, KK_EPS))
        kd_sum = None
        for d in range(2):
            dup_hi, dup_lo = _split2(dup_ref[d, :, lo:hi])
            wlin = w0_ref[d:d + 1, lo:hi] + (_bdot(lw_hi, dup_hi) + (_bdot(lw_hi, dup_lo) + _bdot(lw_lo, dup_hi)))
            lw_out[d, :, lo:hi] = -EXP_NEG_DECAY_BIAS * jax.nn.sigmoid(wlin)
            a = jax.nn.sigmoid(a0_ref[d:d + 1, lo:hi] + _bdot(xa, iup_ref[d, :, lo:hi]))
            kd = k * (1.0 + (a - 1.0) * ka_ref[:, lo:hi])
            kd_out[d, :, lo:hi] = kd.astype(kd_out.dtype)
            be_out[d, :, lo:hi] = (a * kap).astype(be_out.dtype)
            kd_sum = kd if kd_sum is None else kd_sum + kd
        bonus = group_sum(r * kd_sum * rk_ref[:, lo:hi])
        r_out[:, lo:hi] = r.astype(r_out.dtype)
        v_out[:, lo:hi] = v.astype(v_out.dtype)
        kap_out[:, lo:hi] = kap.astype(kap_out.dtype)
        bv_out[:, lo:hi] = (bonus * v).astype(bv_out.dtype)


def _prep(proj, mu_rkv, mu_x, w0, dup, a0, iup, k_k, k_a, r_k, gmat, *, seq, width, xcol_block, tb):
    m = proj.shape[0]
    nsub = tb // HALO
    last_sub = m // HALO - 1
    w3 = 3 * width
    full = lambda shape: pl.BlockSpec(shape, lambda i: (0,) * len(shape))
    prev_idx = lambda i: jnp.maximum(i * nsub - 1, 0)
    next_idx = lambda i: jnp.minimum((i + 1) * nsub, last_sub)
    in_specs = [
        pl.BlockSpec((tb, w3), lambda i: (i, 0)),
        pl.BlockSpec((HALO, w3), lambda i: (prev_idx(i), 0)),
        pl.BlockSpec((HALO, w3), lambda i: (next_idx(i), 0)),
        pl.BlockSpec((tb, 2 * RANK_PAD), lambda i: (i, xcol_block)),
        pl.BlockSpec((HALO, 2 * RANK_PAD), lambda i: (prev_idx(i), xcol_block)),
        pl.BlockSpec((HALO, 2 * RANK_PAD), lambda i: (next_idx(i), xcol_block)),
        full((2, w3)), full((2, 2 * RANK_PAD)),
        full((2, width)), full((2, RANK_PAD, width)), full((2, width)), full((2, RANK_PAD, width)),
        full((1, width)), full((1, width)), full((1, width)), full((SLAB, SLAB)),
    ]
    tok = pl.BlockSpec((tb, width), lambda i: (i, 0))
    tok2 = pl.BlockSpec((2, tb, width), lambda i: (0, i, 0))
    sds = jax.ShapeDtypeStruct((m, width), _BF16)
    sds2 = jax.ShapeDtypeStruct((2, m, width), _BF16)
    lw_sds = jax.ShapeDtypeStruct((2, m, width), _F32)
    return pl.pallas_call(
        functools.partial(_prep_kernel, tb=tb, seq=seq, width=width),
        grid=(m // tb,),
        in_specs=in_specs,
        out_specs=[tok, tok, tok, tok2, tok2, tok2, tok],
        out_shape=[sds, sds, sds, sds2, sds2, lw_sds, sds],
        compiler_params=_cparams(("arbitrary",)),
        name="rwkv_prep",
    )(proj, proj, proj, proj, proj, proj, mu_rkv, mu_x, w0, dup, a0, iup, k_k, k_a, r_k, gmat)


_N_LEVELS = int(np.log2(CHUNK)) - 1
_MASK_INCL, _MASK_STRICT, _MASK_BLK2, _MASK_OFF0 = 0, 1, 2, 3
PACK = MXU_DIM // HEAD_DIM
assert CHUNK == HEAD_DIM and LANE == 2 * HEAD_DIM


def _scan_masks():
    ri, cj = np.indices((CHUNK, CHUNK))
    out = np.zeros((2, 3 + _N_LEVELS, CHUNK, CHUNK), np.float32)
    for d in range(2):
        diff = (ri - cj) * (1 - 2 * d)
        out[d, _MASK_INCL] = diff >= 0
        out[d, _MASK_STRICT] = diff > 0
        out[d, _MASK_BLK2] = (ri >> 1) == (cj >> 1)
        for lv in range(_N_LEVELS):
            k = lv + 1
            out[d, _MASK_OFF0 + lv] = ((ri >> (k + 1)) == (cj >> (k + 1))) & ((ri >> k) != (cj >> k))
    return np.tile(out, (1, 1, 1, PACK))


def _scan_kernel(mask_ref, bd_ref, rf_ref, vf_ref, kapf_ref, kdf_ref, bef_ref, lwf_ref,
                 rb_ref, vb_ref, kapb_ref, kdb_ref, beb_ref, lwb_ref, yf_ref, yb_ref, st_ref, *, groups, rows):
    c = CHUNK
    gw = PACK * HEAD_DIM

    @pl.when(pl.program_id(1) == 0)
    def _():
        st_ref[...] = jnp.zeros_like(st_ref)

    blk2 = mask_ref[0, _MASK_BLK2] > 0
    offs = [mask_ref[0, _MASK_OFF0 + lv] > 0 for lv in range(_N_LEVELS)]
    bdm = bd_ref[...]
    bd_keep = bdm > 0

    keep_lo = bdm[0:c, 0:LANE]
    keep_hi = bdm[HEAD_DIM:HEAD_DIM + c, 0:LANE]
    zero = jnp.zeros((c, LANE), _BF16)

    def bd(x):
        xb = x.astype(_BF16)
        blocks = []
        for t in range(gw // LANE):
            col = xb[:, t * LANE:(t + 1) * LANE]
            for keep in (keep_lo, keep_hi):
                row = [zero] * (gw // LANE)
                row[t] = col * keep
                blocks.append(jnp.concatenate(row, axis=1))
        return jnp.concatenate(blocks, axis=0)

    dir_refs =((rf_ref, vf_ref, kapf_ref, kdf_ref, bef_ref, lwf_ref, yf_ref),
                (rb_ref, vb_ref, kapb_ref, kdb_ref, beb_ref, lwb_ref, yb_ref))
    chains = []
    for d, (r_ref, v_ref, kap_ref, kd_ref, be_ref, lw_ref, y_ref) in enumerate(dir_refs):
        incl = mask_ref[d, _MASK_INCL] > 0
        strict = mask_ref[d, _MASK_STRICT] > 0
        tri = mask_ref[d, _MASK_INCL][:, :c].astype(_BF16)
        for b in range(rows):
            lw = lw_ref[b]
            ci = _dot_exact_rhs(tri, lw, _split3)
            ce = ci - lw
            tot = ci[c - 1:c, :] if d == 0 else ci[0:1, :]
            g_inv = jnp.exp(-ci)
            g_tail = jnp.exp(tot - ci)
            kd = kd_ref[b].astype(_F32)
            be = be_ref[b].astype(_F32)
            kt_all = kap_ref[b].astype(_F32) * jnp.exp(ce)
            rt_all = r_ref[b].astype(_F32) * jnp.exp(ci)
            kh_all = kd * g_inv
            bh_all = be * g_inv
            kb_all = kd * g_tail
            bb_all = be * g_tail
            v_all = v_ref[b]
            g_col = jnp.broadcast_to(jnp.exp(tot), (LANE, tot.shape[1])).T
            for g in range(groups):
                s = slice(g * gw, (g + 1) * gw)
                chains.append(dict(
                    incl=incl, strict=strict, kt=kt_all[:, s], rt=rt_all[:, s], kh=kh_all[:, s],
                    bh=bh_all[:, s], kb=kb_all[:, s], bb=bb_all[:, s], v=v_all[:, s],
                    decay=jnp.concatenate([g_col[s, :]] * (gw // LANE), axis=1),
                    y_ref=y_ref, b=b, sl=s, st=(d * rows + b) * groups + g))

    n = range(len(chains))
    ch = chains
    states = [st_ref[ch[i]["st"]] for i in n]
    kr = [jnp.concatenate([ch[i]["kt"], ch[i]["rt"]], axis=0).astype(_BF16) for i in n]
    ks_rs = [_bdot(kr[i], states[i]) for i in n]
    a_k = [_bdot(kr[i], bd(ch[i]["kh"]), _NT) for i in n]
    a_b = [_bdot(kr[i], bd(ch[i]["bh"]), _NT) for i in n]
    a_kk = [jnp.where(ch[i]["strict"], a_k[i][:c], 0.0) for i in n]
    a_rk = [jnp.where(ch[i]["incl"], a_k[i][c:], 0.0) for i in n]
    lmat = [jnp.where(ch[i]["strict"], a_b[i][:c], 0.0) for i in n]
    a_rb = [jnp.where(ch[i]["incl"], a_b[i][c:], 0.0) for i in n]
    pv = [_bdot(jnp.concatenate([a_kk[i], a_rk[i]], axis=0), bd(ch[i]["v"])) for i in n]
    p = [pv[i][:c] for i in n]
    y_kv = [pv[i][c:] for i in n]

    minv = [-jnp.where(blk2, lmat[i], 0.0) for i in n]
    for off in offs:
        loff = [jnp.where(off, lmat[i], 0.0) for i in n]
        q = [loff[i] + _bdot(minv[i], bd(loff[i])) for i in n]
        minv = [minv[i] - q[i] - _bdot(q[i], bd(minv[i])) for i in n]

    z = [ks_rs[i][:c] + p[i] for i in n]
    u = [z[i] + _bdot(minv[i], bd(z[i])) for i in n]
    y_u = [_bdot(a_rb[i], bd(u[i])) for i in n]
    dst = [_bdot(jnp.concatenate([ch[i]["kb"], -ch[i]["bb"]], axis=0),
                 jnp.concatenate([ch[i]["v"].astype(_BF16), u[i].astype(_BF16)], axis=0), _TN) for i in n]

    for i in n:
        ch[i]["y_ref"][ch[i]["b"], :, ch[i]["sl"]] = (ks_rs[i][c:] + y_kv[i] - y_u[i]).astype(yf_ref.dtype)
        st_ref[ch[i]["st"]] = ch[i]["decay"] * states[i] + jnp.where(bd_keep, dst[i], 0.0)


def _scan(masks, bdmask, r, v, kap, kd, be, lw, *, batch, seq, width, rows):
    nc = seq // CHUNK
    gw = PACK * HEAD_DIM
    groups = width // gw
    to3 = lambda a: a.reshape(batch, seq, width)
    to4 = lambda a: a.reshape(2, batch, seq, width)
    fwd = pl.BlockSpec((rows, CHUNK, width), lambda i, c: (i, c, 0))
    bwd = pl.BlockSpec((rows, CHUNK, width), lambda i, c: (i, nc - 1 - c, 0))
    fwd_d = pl.BlockSpec((None, rows, CHUNK, width), lambda i, c: (0, i, c, 0))
    bwd_d = pl.BlockSpec((None, rows, CHUNK, width), lambda i, c: (1, i, nc - 1 - c, 0))
    r3, v3, kap3, kd4, be4, lw4 = to3(r), to3(v), to3(kap), to4(kd), to4(be), to4(lw)
    sds = jax.ShapeDtypeStruct((batch, seq, width), _BF16)
    yf, yb = pl.pallas_call(
        functools.partial(_scan_kernel, groups=groups, rows=rows),
        grid=(batch // rows, nc),
        in_specs=[pl.BlockSpec(masks.shape, lambda i, c: (0, 0, 0, 0)),
                  pl.BlockSpec((gw, gw), lambda i, c: (0, 0)),
                  fwd, fwd, fwd, fwd_d, fwd_d, fwd_d, bwd, bwd, bwd, bwd_d, bwd_d, bwd_d],
        out_specs=[fwd, bwd],
        out_shape=[sds, sds],
        scratch_shapes=[pltpu.VMEM((2 * rows * groups, gw, gw), _F32)],
        compiler_params=_cparams(("arbitrary", "arbitrary")),
        name="rwkv_scan",
    )(masks, bdmask, r3, v3, kap3, kd4, be4, lw4, r3, v3, kap3, kd4, be4, lw4)
    return yf.reshape(batch * seq, width), yb.reshape(batch * seq, width)


def _band_bias(group):
    kj, qi = np.indices((3 * BLOCK, BLOCK))
    band = np.where(np.abs(kj - BLOCK - qi) <= WINDOW, 0.0, -np.inf).astype(np.float32)
    return np.tile(band, (1, group))


def _attn_kernel(sink_ref, bias_ref, q_ref, kp_ref, km_ref, kn_ref, vp_ref, vm_ref, vn_ref,
                 cp_ref, cm_ref, cn_ref, sp_ref, sm_ref, sn_ref, o_ref, *, nb, group, kv_heads):
    i = pl.program_id(1)
    blk = BLOCK
    scale = HEAD_DIM ** -0.5 * LOG2E
    q = _rope(q_ref[...].astype(_F32), cm_ref[...], sm_ref[...]) * scale
    k4 = jnp.concatenate([_rope(kp_ref[...].astype(_F32), cp_ref[...], sp_ref[...]),
                          _rope(km_ref[...].astype(_F32), cm_ref[...], sm_ref[...]),
                          _rope(kn_ref[...].astype(_F32), cn_ref[...], sn_ref[...])], axis=0)
    v4 = jnp.concatenate([vp_ref[...], vm_ref[...], vn_ref[...]], axis=0)

    no_prev = jnp.where(i == 0, -jnp.inf, 0.0)
    no_next = jnp.where(2 * i + 1 == nb - 1, -jnp.inf, 0.0)
    biases = (jnp.concatenate([bias_ref[0:blk, :] + no_prev, bias_ref[blk:, :]], axis=0),
              jnp.concatenate([bias_ref[0:2 * blk, :], bias_ref[2 * blk:, :] + no_next], axis=0))
    head_of_lane = lax.broadcasted_iota(jnp.int32, (1, group * blk), 1) // blk

    chains = [(a, g) for a in range(2) for g in range(kv_heads)]
    sk = []
    for g in range(kv_heads):
        row = jnp.zeros((1, group * blk), _F32)
        for j in range(group):
            row = jnp.where(head_of_lane == j, sink_ref[g * group + j] * LOG2E, row)
        sk.append(row)
    s_t, o_t = [], {}
    for a, g in chains:
        qs = jnp.concatenate([q[a * blk:(a + 1) * blk, (g * group + j) * HEAD_DIM:(g * group + j + 1) * HEAD_DIM]
                              for j in range(group)], axis=0)
        keys = k4[a * blk:(a + 3) * blk, g * HEAD_DIM:(g + 1) * HEAD_DIM]
        s_t.append(_bdot(keys, qs, _NT) + biases[a])
    for c, (a, g) in enumerate(chains):
        mx = jnp.maximum(jnp.max(s_t[c], axis=0, keepdims=True), sk[g])
        p_t = jnp.exp2(s_t[c] - mx)
        denom = jnp.sum(p_t, axis=0, keepdims=True) + jnp.exp2(sk[g] - mx)
        vals = v4[a * blk:(a + 3) * blk, g * HEAD_DIM:(g + 1) * HEAD_DIM]
        o_t[a, g] = _bdot(vals, p_t, _TN) / denom
    for a in range(2):
        for t in range(kv_heads * group // 2):
            g, j = divmod(2 * t, group)
            pair = jnp.concatenate([o_t[a, g][:, j * blk:(j + 1) * blk],
                                    o_t[a, g][:, (j + 1) * blk:(j + 2) * blk]], axis=0)
            o_ref[a * blk:(a + 1) * blk, t * LANE:(t + 1) * LANE] = pair.T.astype(o_ref.dtype)


def _attention(proj, cos_t, sin_t, sink_l, *, batch, seq, at_width, kv_width, q_block, k_block, v_block):
    m = proj.shape[0]
    nb = seq // BLOCK
    kv_heads = kv_width // HEAD_DIM
    group = at_width // kv_width
    assert group % 2 == 0 and BLOCK == LANE
    bias = jnp.asarray(_band_bias(group))
    assert nb % 2 == 0
    half = nb // 2
    mid = (2 * BLOCK, lambda b, i: b * half + i)
    prv = (BLOCK, lambda b, i: b * nb + jnp.maximum(2 * i - 1, 0))
    nxt = (BLOCK, lambda b, i: b * nb + jnp.minimum(2 * i + 2, nb - 1))

    def spec(width, where, colblock):
        rows, rowfn = where
        return pl.BlockSpec((rows, width), lambda b, i: (rowfn(b, i), colblock))

    in_specs = [pl.BlockSpec(memory_space=pltpu.SMEM),
                pl.BlockSpec(bias.shape, lambda b, i: (0, 0)), spec(at_width, mid, q_block)]
    in_specs += [spec(kv_width, f, k_block) for f in (prv, mid, nxt)]
    in_specs += [spec(kv_width, f, v_block) for f in (prv, mid, nxt)]
    in_specs += [spec(LANE, f, 0) for f in (prv, mid, nxt)] * 2
    return pl.pallas_call(
        functools.partial(_attn_kernel, nb=nb, group=group, kv_heads=kv_heads),
        grid=(batch, half),
        in_specs=in_specs,
        out_specs=spec(at_width, mid, 0),
        out_shape=jax.ShapeDtypeStruct((m, at_width), _BF16),
        compiler_params=_cparams(("arbitrary", "arbitrary")),
        name="window_attention",
    )(sink_l, bias, proj, proj, proj, proj, proj, proj, proj, cos_t, cos_t, cos_t, sin_t, sin_t, sin_t)


def _silu(g):
    return g * jax.nn.sigmoid(g)


def _outproj_kernel(yf_ref, yb_ref, bv_ref, grw_ref, yat_ref, gat_ref, x_ref, lg_ref, lb_ref, g_ref,
                    wrw_ref, wat_ref, fg_ref, o_ref, mix_ref, *, width, final):
    gmat = g_ref[...]
    inv_n = 1.0 / HEAD_DIM

    def group_mean(z):
        return _bdot(z, gmat) * inv_n

    for s in range(width // SLAB):
        lo, hi = s * SLAB, (s + 1) * SLAB
        y = yf_ref[:, lo:hi].astype(_F32) + yb_ref[:, lo:hi].astype(_F32)
        yc = y - group_mean(y)
        var = group_mean(yc * yc)
        yn = yc * lax.rsqrt(var + LNX_EPS) * lg_ref[:, lo:hi] + lb_ref[:, lo:hi]
        gate = _silu(grw_ref[:, lo:hi].astype(_F32))
        mix_ref[:, lo:hi] = ((yn + bv_ref[:, lo:hi].astype(_F32)) * gate).astype(_BF16)
    mix_at = (yat_ref[...].astype(_F32) * _silu(gat_ref[...].astype(_F32))).astype(_BF16)
    out = x_ref[...] + (jnp.dot(mix_ref[...], wrw_ref[...], preferred_element_type=_F32)
                        + jnp.dot(mix_at, wat_ref[...], preferred_element_type=_F32))
    if final:
        out = out * lax.rsqrt(jnp.mean(out * out, axis=-1, keepdims=True) + NORM_EPS) * fg_ref[...]
    o_ref[...] = out


def _outproj(yf, yb, bv, proj, yat, x2, lnx_g, lnx_b, gmat, w_out, layer, final_g, *, width, at_width,
             grw_block, gat_block, final, tm):
    m, d = x2.shape
    assert width == at_width
    full = lambda shape: pl.BlockSpec(shape, lambda i: (0,) * len(shape))
    in_specs = [
        pl.BlockSpec((tm, width), lambda i: (i, 0)),
        pl.BlockSpec((tm, width), lambda i: (i, 0)),
        pl.BlockSpec((tm, width), lambda i: (i, 0)),
        pl.BlockSpec((tm, width), lambda i: (i, grw_block)),
        pl.BlockSpec((tm, at_width), lambda i: (i, 0)),
        pl.BlockSpec((tm, at_width), lambda i: (i, gat_block)),
        pl.BlockSpec((tm, d), lambda i: (i, 0)),
        full((1, width)), full((1, width)), full((SLAB, SLAB)),
        pl.BlockSpec((None, width, d), lambda i: (layer, 0, 0)),
        pl.BlockSpec((None, at_width, d), lambda i: (layer, 1, 0)), full((1, d)),
    ]
    return pl.pallas_call(
        functools.partial(_outproj_kernel, width=width, final=final),
        grid=(m // tm,),
        in_specs=in_specs,
        out_specs=pl.BlockSpec((tm, d), lambda i: (i, 0)),
        out_shape=jax.ShapeDtypeStruct((m, d), _F32),
        scratch_shapes=[pltpu.VMEM((tm, width), _BF16)],
        compiler_params=_cparams(("arbitrary",)),
        name="gate_outproj",
    )(yf, yb, bv, proj, yat, proj, x2, lnx_g, lnx_b, gmat, w_out, w_out, final_g)


def _tile_plan(m, batch):
    return dict(inproj_tm=min(m, 512), inproj_tn=9 * MXU_DIM, prep_tb=min(m, 256), out_tm=min(m, 256),
                scan_rows=2 if batch % 2 == 0 else 1)


def kernel(x, positions, norm_g, w_in, shift_mu, w0, decay_up, a0, iclr_up, k_k, k_a, r_k, lnx_g, lnx_b,
           sink, w_out, final_g):
    batch, seq, d_model = x.shape
    depth = w_in.shape[0]
    width = k_k.shape[-1]
    at_width = sink.shape[-1] * HEAD_DIM
    rank = decay_up.shape[2]
    in_cols = w_in.shape[-1]
    kv_width = (in_cols - (3 * width + 2 * rank) - width - 2 * at_width) // 2
    m = batch * seq
    assert iclr_up.shape[2] == rank and rank <= RANK_PAD
    assert width % SLAB == 0 and at_width == width and kv_width % LANE == 0
    assert seq % BLOCK == 0 and seq % CHUNK == 0 and BLOCK == WINDOW
    plan = _tile_plan(m, batch)
    assert m % plan["inproj_tm"] == 0 and seq % plan["prep_tb"] == 0 and m % plan["out_tm"] == 0

    o_xw = 3 * width
    o_xa = o_xw + rank
    o_grw = o_xa + rank
    o_q = o_grw + width
    o_k = o_q + at_width
    o_v = o_k + kv_width
    o_gat = o_v + kv_width
    pad_cols = lambda a: jnp.pad(a, [(0, 0)] * (a.ndim - 1) + [(0, RANK_PAD - rank)])
    sources = [(0, o_xw), (o_grw, o_q), (o_q, o_k), (o_gat, in_cols), (o_k, o_v), (o_v, o_gat)]
    sections, dst = [], 0
    for lo, hi in sources:
        sections.append((lo, hi, dst))
        dst += hi - lo
    for lo, hi in ((o_xw, o_xa), (o_xa, o_grw)):
        sections.append((lo, hi, dst))
        dst += RANK_PAD
    n_cols = dst
    w_perm = _relayout_weight(w_in, tuple(sections), n_cols)
    assert n_cols % plan["inproj_tn"] == 0
    grw_block = (3 * width) // width
    q_block = grw_block + 1
    gat_block = q_block + 1
    kcol = 3 * width + width + 2 * at_width
    assert kcol % kv_width == 0 and (kcol + 2 * kv_width) % (2 * RANK_PAD) == 0
    k_block = kcol // kv_width
    v_block = k_block + 1
    xcol_block = (kcol + 2 * kv_width) // (2 * RANK_PAD)

    mu_rkv = shift_mu[..., :o_xw]
    mu_x = jnp.concatenate([pad_cols(shift_mu[..., o_xw:o_xa]), pad_cols(shift_mu[..., o_xa:o_grw])], axis=-1)
    pad_rows = lambda a: jnp.pad(a, [(0, 0), (0, 0), (0, RANK_PAD - rank), (0, 0)])
    dup = pad_rows(decay_up)
    iup = pad_rows(iclr_up)
    rk_flat = r_k.reshape(depth, 1, width)
    w_out_bf = w_out.astype(_BF16)

    lane_head = np.arange(SLAB) // HEAD_DIM
    gmat = jnp.asarray(lane_head[:, None] == lane_head[None, :], _BF16)
    masks = jnp.asarray(_scan_masks())

    x2 = x.reshape(m, d_model)
    cos_t, sin_t = _rope_tables(positions, m)
    for l in range(depth):
        proj = _inproj(x2, norm_g[l], w_perm, l, plan["inproj_tm"], plan["inproj_tn"])
        r, v, kap, kd, be, lw, bv = _prep(
            proj, mu_rkv[l], mu_x[l], w0[l], dup[l], a0[l], iup[l], k_k[l].reshape(1, width),
            k_a[l].reshape(1, width), rk_flat[l], gmat, seq=seq, width=width, xcol_block=xcol_block,
            tb=plan["prep_tb"])
        yf, yb = _scan(masks, gmat, r, v, kap, kd, be, lw, batch=batch, seq=seq, width=width,
                       rows=plan["scan_rows"])
        yat = _attention(proj, cos_t, sin_t, sink[l], batch=batch, seq=seq, at_width=at_width,
                         kv_width=kv_width, q_block=q_block, k_block=k_block, v_block=v_block)
        x2 = _outproj(yf, yb, bv, proj, yat, x2, lnx_g[l].reshape(1, width), lnx_b[l].reshape(1, width), gmat,
                      w_out_bf, l, final_g.reshape(1, d_model), width=width,
                      at_width=at_width, grw_block=grw_block, gat_block=gat_block,
                      final=(l == depth - 1), tm=plan["out_tm"])
    return x2.reshape(batch, seq, d_model)
```

```python
import functools
import math

import numpy as np
import jax
import jax.numpy as jnp
from jax import lax
from jax.experimental import pallas as pl
from jax.experimental.pallas import tpu as pltpu

HEAD_DIM = 64
WINDOW = 128
BLOCK = 128
ROPE_THETA = 10000.0
NORM_EPS = 1e-6
LNX_EPS = 64e-5
KK_EPS = 1e-24
DECAY_BIAS = 0.5
LOG2E = math.log2(math.e)
EXP_NEG_DECAY_BIAS = math.exp(-DECAY_BIAS)

LANE = 128
SUBLANE = 8
HALO = 2 * SUBLANE
MXU_DIM = 256
VMEM_LIMIT_BYTES = 56 * 1024 * 1024

CHUNK = 64
SLAB = MXU_DIM
RANK_PAD = LANE

_F32 = jnp.float32
_BF16 = jnp.bfloat16
_NN = (((1,), (0,)), ((), ()))
_NT = (((1,), (1,)), ((), ()))
_TN = (((0,), (0,)), ((), ()))


def _bdot(a, b, dims=_NN):
    return lax.dot_general(a.astype(_BF16), b.astype(_BF16), dims, preferred_element_type=_F32)


def _split2(x):
    hi = x.astype(_BF16)
    lo = (x - hi.astype(_F32)).astype(_BF16)
    return hi, lo


def _split3(x):
    hi = x.astype(_BF16)
    r1 = x - hi.astype(_F32)
    mid = r1.astype(_BF16)
    lo = (r1 - mid.astype(_F32)).astype(_BF16)
    return hi, mid, lo


def _dot_exact_rhs(a_bf16, x, parts):
    out = None
    for p in parts(x):
        t = _bdot(a_bf16, p)
        out = t if out is None else out + t
    return out


def _cparams(semantics):
    return pltpu.CompilerParams(dimension_semantics=semantics, vmem_limit_bytes=VMEM_LIMIT_BYTES)


def _rope_table_kernel(pos_ref, inv_ref, cos_ref, sin_ref):
    ang = pos_ref[...].astype(_F32) * inv_ref[...]
    lane = lax.broadcasted_iota(jnp.int32, ang.shape, 1)
    first_half = (lane & (HEAD_DIM - 1)) < (HEAD_DIM // 2)
    s = jnp.sin(ang)
    cos_ref[...] = jnp.cos(ang)
    sin_ref[...] = jnp.where(first_half, -s, s)


def _rope_tables(positions, m):
    half = HEAD_DIM // 2
    inv = jnp.power(ROPE_THETA, -jnp.arange(half, dtype=_F32) / half)
    inv_row = jnp.tile(inv, LANE // half).reshape(1, LANE)
    tb = min(m, 2048)
    return pl.pallas_call(
        _rope_table_kernel,
        grid=(m // tb,),
        in_specs=[pl.BlockSpec((tb, 1), lambda i: (i, 0)), pl.BlockSpec((1, LANE), lambda i: (0, 0))],
        out_specs=[pl.BlockSpec((tb, LANE), lambda i: (i, 0))] * 2,
        out_shape=[jax.ShapeDtypeStruct((m, LANE), _F32)] * 2,
        compiler_params=_cparams(("arbitrary",)),
        name="rope_tables",
    )(positions.reshape(m, 1), inv_row)


def _rope(t, cos, sin):
    w = t.shape[1]
    half = HEAD_DIM // 2
    lane = lax.broadcasted_iota(jnp.int32, t.shape, 1)
    first_half = (lane & (HEAD_DIM - 1)) < half
    partner = jnp.where(first_half, pltpu.roll(t, w - half, 1), pltpu.roll(t, half, 1))
    reps = w // LANE
    return t * jnp.tile(cos, (1, reps)) + partner * jnp.tile(sin, (1, reps))


def _relayout_kernel(w_ref, o_ref, *, sections):
    covered = 0
    for lo, hi, dst in sections:
        if dst > covered:
            o_ref[covered:dst, :] = jnp.zeros((dst - covered, o_ref.shape[1]), o_ref.dtype)
        o_ref[dst:dst + hi - lo, :] = w_ref[lo:hi, :].astype(o_ref.dtype)
        covered = dst + hi - lo
    if covered < o_ref.shape[0]:
        o_ref[covered:, :] = jnp.zeros((o_ref.shape[0] - covered, o_ref.shape[1]), o_ref.dtype)


def _relayout_weight(w_t, sections, n_rows):
    depth, rows, cols = w_t.shape
    tc = min(cols, MXU_DIM)
    return pl.pallas_call(
        functools.partial(_relayout_kernel, sections=sections),
        grid=(depth, cols // tc),
        in_specs=[pl.BlockSpec((None, rows, tc), lambda l, i: (l, 0, i))],
        out_specs=pl.BlockSpec((None, n_rows, tc), lambda l, i: (l, 0, i)),
        out_shape=jax.ShapeDtypeStruct((depth, n_rows, cols), _BF16),
        compiler_params=_cparams(("arbitrary", "arbitrary")),
        name="weight_layout",
    )(w_t)


def _inproj_kernel(x_ref, g_ref, w_ref, o_ref, h_ref, *, tn):
    rows = min(x_ref.shape[0], 256)
    for r0 in range(0, x_ref.shape[0], rows):
        x = x_ref[r0:r0 + rows, :]
        y = x * lax.rsqrt(jnp.mean(x * x, axis=-1, keepdims=True) + NORM_EPS)
        h_ref[r0:r0 + rows, :] = (y * g_ref[...]).astype(_BF16)

    def col_tile(j, carry):
        cols = pl.ds(pl.multiple_of(j * tn, tn), tn)
        tile = lax.dot_general(h_ref[...], w_ref[cols, :], _NT, preferred_element_type=_F32)
        o_ref[:, cols] = tile.astype(o_ref.dtype)
        return carry

    lax.fori_loop(0, w_ref.shape[0] // tn, col_tile, 0)


def _inproj(x2, g, w_t, layer, tm, tn):
    m, d = x2.shape
    n = w_t.shape[1]
    w = w_t
    return pl.pallas_call(
        functools.partial(_inproj_kernel, tn=tn),
        grid=(m // tm,),
        in_specs=[
            pl.BlockSpec((tm, d), lambda i: (i, 0)),
            pl.BlockSpec((1, d), lambda i: (0, 0)),
            pl.BlockSpec((None, n, d), lambda i: (layer, 0, 0), pipeline_mode=pl.Buffered(1)),
        ],
        out_specs=pl.BlockSpec((tm, n), lambda i: (i, 0)),
        out_shape=jax.ShapeDtypeStruct((m, n), _BF16),
        scratch_shapes=[pltpu.VMEM((tm, d), _BF16)],
        compiler_params=_cparams(("arbitrary",)),
        name="norm_inproj",
    )(x2, g.reshape(1, d), w)


def _prep_kernel(rkv_ref, rkvp_ref, rkvn_ref, xw_ref, xwp_ref, xwn_ref, murkv_ref, mux_ref,
                 w0_ref, dup_ref, a0_ref, iup_ref, kk_ref, ka_ref, rk_ref, g_ref,
                 r_out, v_out, kap_out, kd_out, be_out, lw_out, bv_out, *, tb, seq, width):
    i = pl.program_id(0)
    t0 = lax.rem(i * tb, seq)
    first = t0 == 0
    last = t0 + tb == seq
    row = lax.broadcasted_iota(jnp.int32, (tb, 1), 0)
    is_first_row = row == 0
    is_last_row = row == tb - 1
    gmat = g_ref[...]

    def shifted(main, prv, nxt, mu, lo, hi):
        u = main[:, lo:hi].astype(_F32)
        p_row = jnp.where(first, 0.0, prv[HALO - 1:HALO, lo:hi].astype(_F32))
        n_row = jnp.where(last, 0.0, nxt[0:1, lo:hi].astype(_F32))
        up = jnp.where(is_first_row, p_row, pltpu.roll(u, 1, 0))
        un = jnp.where(is_last_row, n_row, pltpu.roll(u, tb - 1, 0))
        mu0, mu1 = mu[0:1, lo:hi], mu[1:2, lo:hi]
        return (1.0 - mu0 - mu1) * u + mu0 * up + mu1 * un

    def group_sum(z):
        return _bdot(z, gmat)

    xs = shifted(xw_ref, xwp_ref, xwn_ref, mux_ref, 0, 2 * RANK_PAD)
    lw_hi, lw_lo = _split2(jnp.tanh(xs[:, :RANK_PAD]))
    xa = xs[:, RANK_PAD:].astype(_BF16)

    for s in range(width // SLAB):
        lo, hi = s * SLAB, (s + 1) * SLAB
        r = shifted(rkv_ref, rkvp_ref, rkvn_ref, murkv_ref, lo, hi)
        k = shifted(rkv_ref, rkvp_ref, rkvn_ref, murkv_ref, width + lo, width + hi)
        v = shifted(rkv_ref, rkvp_ref, rkvn_ref, murkv_ref, 2 * width + lo, 2 * width + hi)
        kkv = k * kk_ref[:, lo:hi]
        ss = group_sum(kkv * kkv)
        kap = kkv * lax.rsqrt(jnp.maximum(ss, KK_EPS))
        kd_sum = None
        for d in range(2):
            dup_hi, dup_lo = _split2(dup_ref[d, :, lo:hi])
            wlin = w0_ref[d:d + 1, lo:hi] + (_bdot(lw_hi, dup_hi) + (_bdot(lw_hi, dup_lo) + _bdot(lw_lo, dup_hi)))
            lw_out[d, :, lo:hi] = -EXP_NEG_DECAY_BIAS * jax.nn.sigmoid(wlin)
            a = jax.nn.sigmoid(a0_ref[d:d + 1, lo:hi] + _bdot(xa, iup_ref[d, :, lo:hi]))
            kd = k * (1.0 + (a - 1.0) * ka_ref[:, lo:hi])
            kd_out[d, :, lo:hi] = kd.astype(kd_out.dtype)
            be_out[d, :, lo:hi] = (a * kap).astype(be_out.dtype)
            kd_sum = kd if kd_sum is None else kd_sum + kd
        bonus = group_sum(r * kd_sum * rk_ref[:, lo:hi])
        r_out[:, lo:hi] = r.astype(r_out.dtype)
        v_out[:, lo:hi] = v.astype(v_out.dtype)
        kap_out[:, lo:hi] = kap.astype(kap_out.dtype)
        bv_out[:, lo:hi] = (bonus * v).astype(bv_out.dtype)


def _prep(proj, mu_rkv, mu_x, w0, dup, a0, iup, k_k, k_a, r_k, gmat, *, seq, width, xcol_block, tb):
    m = proj.shape[0]
    nsub = tb // HALO
    last_sub = m // HALO - 1
    w3 = 3 * width
    full = lambda shape: pl.BlockSpec(shape, lambda i: (0,) * len(shape))
    prev_idx = lambda i: jnp.maximum(i * nsub - 1, 0)
    next_idx = lambda i: jnp.minimum((i + 1) * nsub, last_sub)
    in_specs = [
        pl.BlockSpec((tb, w3), lambda i: (i, 0)),
        pl.BlockSpec((HALO, w3), lambda i: (prev_idx(i), 0)),
        pl.BlockSpec((HALO, w3), lambda i: (next_idx(i), 0)),
        pl.BlockSpec((tb, 2 * RANK_PAD), lambda i: (i, xcol_block)),
        pl.BlockSpec((HALO, 2 * RANK_PAD), lambda i: (prev_idx(i), xcol_block)),
        pl.BlockSpec((HALO, 2 * RANK_PAD), lambda i: (next_idx(i), xcol_block)),
        full((2, w3)), full((2, 2 * RANK_PAD)),
        full((2, width)), full((2, RANK_PAD, width)), full((2, width)), full((2, RANK_PAD, width)),
        full((1, width)), full((1, width)), full((1, width)), full((SLAB, SLAB)),
    ]
    tok = pl.BlockSpec((tb, width), lambda i: (i, 0))
    tok2 = pl.BlockSpec((2, tb, width), lambda i: (0, i, 0))
    sds = jax.ShapeDtypeStruct((m, width), _BF16)
    sds2 = jax.ShapeDtypeStruct((2, m, width), _BF16)
    lw_sds = jax.ShapeDtypeStruct((2, m, width), _F32)
    return pl.pallas_call(
        functools.partial(_prep_kernel, tb=tb, seq=seq, width=width),
        grid=(m // tb,),
        in_specs=in_specs,
        out_specs=[tok, tok, tok, tok2, tok2, tok2, tok],
        out_shape=[sds, sds, sds, sds2, sds2, lw_sds, sds],
        compiler_params=_cparams(("arbitrary",)),
        name="rwkv_prep",
    )(proj, proj, proj, proj, proj, proj, mu_rkv, mu_x, w0, dup, a0, iup, k_k, k_a, r_k, gmat)


_N_LEVELS = int(np.log2(CHUNK)) - 1
_MASK_INCL, _MASK_STRICT, _MASK_BLK2, _MASK_OFF0 = 0, 1, 2, 3
PACK = MXU_DIM // HEAD_DIM
assert CHUNK == HEAD_DIM and LANE == 2 * HEAD_DIM


def _scan_masks():
    ri, cj = np.indices((CHUNK, CHUNK))
    out = np.zeros((2, 3 + _N_LEVELS, CHUNK, CHUNK), np.float32)
    for d in range(2):
        diff = (ri - cj) * (1 - 2 * d)
        out[d, _MASK_INCL] = diff >= 0
        out[d, _MASK_STRICT] = diff > 0
        out[d, _MASK_BLK2] = (ri >> 1) == (cj >> 1)
        for lv in range(_N_LEVELS):
            k = lv + 1
            out[d, _MASK_OFF0 + lv] = ((ri >> (k + 1)) == (cj >> (k + 1))) & ((ri >> k) != (cj >> k))
    return np.tile(out, (1, 1, 1, PACK))


def _scan_kernel(mask_ref, bd_ref, rf_ref, vf_ref, kapf_ref, kdf_ref, bef_ref, lwf_ref,
                 rb_ref, vb_ref, kapb_ref, kdb_ref, beb_ref, lwb_ref, yf_ref, yb_ref, st_ref, *, groups, rows):
    c = CHUNK
    gw = PACK * HEAD_DIM

    @pl.when(pl.program_id(1) == 0)
    def _():
        st_ref[...] = jnp.zeros_like(st_ref)

    blk2 = mask_ref[0, _MASK_BLK2] > 0
    offs = [mask_ref[0, _MASK_OFF0 + lv] > 0 for lv in range(_N_LEVELS)]
    bdm = bd_ref[...]
    bd_keep = bdm > 0

    keep_lo = bdm[0:c, 0:LANE]
    keep_hi = bdm[HEAD_DIM:HEAD_DIM + c, 0:LANE]
    zero = jnp.zeros((c, LANE), _BF16)

    def bd(x):
        xb = x.astype(_BF16)
        blocks = []
        for t in range(gw // LANE):
            col = xb[:, t * LANE:(t + 1) * LANE]
            for keep in (keep_lo, keep_hi):
                row = [zero] * (gw // LANE)
                row[t] = col * keep
                blocks.append(jnp.concatenate(row, axis=1))
        return jnp.concatenate(blocks, axis=0)

    dir_refs =((rf_ref, vf_ref, kapf_ref, kdf_ref, bef_ref, lwf_ref, yf_ref),
                (rb_ref, vb_ref, kapb_ref, kdb_ref, beb_ref, lwb_ref, yb_ref))
    chains = []
    for d, (r_ref, v_ref, kap_ref, kd_ref, be_ref, lw_ref, y_ref) in enumerate(dir_refs):
        incl = mask_ref[d, _MASK_INCL] > 0
        strict = mask_ref[d, _MASK_STRICT] > 0
        tri = mask_ref[d, _MASK_INCL][:, :c].astype(_BF16)
        for b in range(rows):
            lw = lw_ref[b]
            ci = _dot_exact_rhs(tri, lw, _split3)
            ce = ci - lw
            tot = ci[c - 1:c, :] if d == 0 else ci[0:1, :]
            g_inv = jnp.exp(-ci)
            g_tail = jnp.exp(tot - ci)
            kd = kd_ref[b].astype(_F32)
            be = be_ref[b].astype(_F32)
            kt_all = kap_ref[b].astype(_F32) * jnp.exp(ce)
            rt_all = r_ref[b].astype(_F32) * jnp.exp(ci)
            kh_all = kd * g_inv
            bh_all = be * g_inv
            kb_all = kd * g_tail
            bb_all = be * g_tail
            v_all = v_ref[b]
            g_col = jnp.broadcast_to(jnp.exp(tot), (LANE, tot.shape[1])).T
            for g in range(groups):
                s = slice(g * gw, (g + 1) * gw)
                chains.append(dict(
                    incl=incl, strict=strict, kt=kt_all[:, s], rt=rt_all[:, s], kh=kh_all[:, s],
                    bh=bh_all[:, s], kb=kb_all[:, s], bb=bb_all[:, s], v=v_all[:, s],
                    decay=jnp.concatenate([g_col[s, :]] * (gw // LANE), axis=1),
                    y_ref=y_ref, b=b, sl=s, st=(d * rows + b) * groups + g))

    n = range(len(chains))
    ch = chains
    states = [st_ref[ch[i]["st"]] for i in n]
    kr = [jnp.concatenate([ch[i]["kt"], ch[i]["rt"]], axis=0).astype(_BF16) for i in n]
    ks_rs = [_bdot(kr[i], states[i]) for i in n]
    a_k = [_bdot(kr[i], bd(ch[i]["kh"]), _NT) for i in n]
    a_b = [_bdot(kr[i], bd(ch[i]["bh"]), _NT) for i in n]
    a_kk = [jnp.where(ch[i]["strict"], a_k[i][:c], 0.0) for i in n]
    a_rk = [jnp.where(ch[i]["incl"], a_k[i][c:], 0.0) for i in n]
    lmat = [jnp.where(ch[i]["strict"], a_b[i][:c], 0.0) for i in n]
    a_rb = [jnp.where(ch[i]["incl"], a_b[i][c:], 0.0) for i in n]
    pv = [_bdot(jnp.concatenate([a_kk[i], a_rk[i]], axis=0), bd(ch[i]["v"])) for i in n]
    p = [pv[i][:c] for i in n]
    y_kv = [pv[i][c:] for i in n]

    minv = [-jnp.where(blk2, lmat[i], 0.0) for i in n]
    for off in offs:
        loff = [jnp.where(off, lmat[i], 0.0) for i in n]
        q = [loff[i] + _bdot(minv[i], bd(loff[i])) for i in n]
        minv = [minv[i] - q[i] - _bdot(q[i], bd(minv[i])) for i in n]

    z = [ks_rs[i][:c] + p[i] for i in n]
    u = [z[i] + _bdot(minv[i], bd(z[i])) for i in n]
    y_u = [_bdot(a_rb[i], bd(u[i])) for i in n]
    dst = [_bdot(jnp.concatenate([ch[i]["kb"], -ch[i]["bb"]], axis=0),
                 jnp.concatenate([ch[i]["v"].astype(_BF16), u[i].astype(_BF16)], axis=0), _TN) for i in n]

    for i in n:
        ch[i]["y_ref"][ch[i]["b"], :, ch[i]["sl"]] = (ks_rs[i][c:] + y_kv[i] - y_u[i]).astype(yf_ref.dtype)
        st_ref[ch[i]["st"]] = ch[i]["decay"] * states[i] + jnp.where(bd_keep, dst[i], 0.0)


def _scan(masks, bdmask, r, v, kap, kd, be, lw, *, batch, seq, width, rows):
    nc = seq // CHUNK
    gw = PACK * HEAD_DIM
    groups = width // gw
    to3 = lambda a: a.reshape(batch, seq, width)
    to4 = lambda a: a.reshape(2, batch, seq, width)
    fwd = pl.BlockSpec((rows, CHUNK, width), lambda i, c: (i, c, 0))
    bwd = pl.BlockSpec((rows, CHUNK, width), lambda i, c: (i, nc - 1 - c, 0))
    fwd_d = pl.BlockSpec((None, rows, CHUNK, width), lambda i, c: (0, i, c, 0))
    bwd_d = pl.BlockSpec((None, rows, CHUNK, width), lambda i, c: (1, i, nc - 1 - c, 0))
    r3, v3, kap3, kd4, be4, lw4 = to3(r), to3(v), to3(kap), to4(kd), to4(be), to4(lw)
    sds = jax.ShapeDtypeStruct((batch, seq, width), _BF16)
    yf, yb = pl.pallas_call(
        functools.partial(_scan_kernel, groups=groups, rows=rows),
        grid=(batch // rows, nc),
        in_specs=[pl.BlockSpec(masks.shape, lambda i, c: (0, 0, 0, 0)),
                  pl.BlockSpec((gw, gw), lambda i, c: (0, 0)),
                  fwd, fwd, fwd, fwd_d, fwd_d, fwd_d, bwd, bwd, bwd, bwd_d, bwd_d, bwd_d],
        out_specs=[fwd, bwd],
        out_shape=[sds, sds],
        scratch_shapes=[pltpu.VMEM((2 * rows * groups, gw, gw), _F32)],
        compiler_params=_cparams(("arbitrary", "arbitrary")),
        name="rwkv_scan",
    )(masks, bdmask, r3, v3, kap3, kd4, be4, lw4, r3, v3, kap3, kd4, be4, lw4)
    return yf.reshape(batch * seq, width), yb.reshape(batch * seq, width)


def _band_bias(group):
    kj, qi = np.indices((3 * BLOCK, BLOCK))
    band = np.where(np.abs(kj - BLOCK - qi) <= WINDOW, 0.0, -np.inf).astype(np.float32)
    return np.tile(band, (1, group))


def _attn_kernel(sink_ref, bias_ref, q_ref, kp_ref, km_ref, kn_ref, vp_ref, vm_ref, vn_ref,
                 cp_ref, cm_ref, cn_ref, sp_ref, sm_ref, sn_ref, o_ref, *, nb, group, kv_heads):
    i = pl.program_id(1)
    blk = BLOCK
    scale = HEAD_DIM ** -0.5 * LOG2E
    q = _rope(q_ref[...].astype(_F32), cm_ref[...], sm_ref[...]) * scale
    k4 = jnp.concatenate([_rope(kp_ref[...].astype(_F32), cp_ref[...], sp_ref[...]),
                          _rope(km_ref[...].astype(_F32), cm_ref[...], sm_ref[...]),
                          _rope(kn_ref[...].astype(_F32), cn_ref[...], sn_ref[...])], axis=0)
    v4 = jnp.concatenate([vp_ref[...], vm_ref[...], vn_ref[...]], axis=0)

    no_prev = jnp.where(i == 0, -jnp.inf, 0.0)
    no_next = jnp.where(2 * i + 1 == nb - 1, -jnp.inf, 0.0)
    biases = (jnp.concatenate([bias_ref[0:blk, :] + no_prev, bias_ref[blk:, :]], axis=0),
              jnp.concatenate([bias_ref[0:2 * blk, :], bias_ref[2 * blk:, :] + no_next], axis=0))
    head_of_lane = lax.broadcasted_iota(jnp.int32, (1, group * blk), 1) // blk

    chains = [(a, g) for a in range(2) for g in range(kv_heads)]
    sk = []
    for g in range(kv_heads):
        row = jnp.zeros((1, group * blk), _F32)
        for j in range(group):
            row = jnp.where(head_of_lane == j, sink_ref[g * group + j] * LOG2E, row)
        sk.append(row)
    s_t, o_t = [], {}
    for a, g in chains:
        qs = jnp.concatenate([q[a * blk:(a + 1) * blk, (g * group + j) * HEAD_DIM:(g * group + j + 1) * HEAD_DIM]
                              for j in range(group)], axis=0)
        keys = k4[a * blk:(a + 3) * blk, g * HEAD_DIM:(g + 1) * HEAD_DIM]
        s_t.append(_bdot(keys, qs, _NT) + biases[a])
    for c, (a, g) in enumerate(chains):
        mx = jnp.maximum(jnp.max(s_t[c], axis=0, keepdims=True), sk[g])
        p_t = jnp.exp2(s_t[c] - mx)
        denom = jnp.sum(p_t, axis=0, keepdims=True) + jnp.exp2(sk[g] - mx)
        vals = v4[a * blk:(a + 3) * blk, g * HEAD_DIM:(g + 1) * HEAD_DIM]
        o_t[a, g] = _bdot(vals, p_t, _TN) / denom
    for a in range(2):
        for t in range(kv_heads * group // 2):
            g, j = divmod(2 * t, group)
            pair = jnp.concatenate([o_t[a, g][:, j * blk:(j + 1) * blk],
                                    o_t[a, g][:, (j + 1) * blk:(j + 2) * blk]], axis=0)
            o_ref[a * blk:(a + 1) * blk, t * LANE:(t + 1) * LANE] = pair.T.astype(o_ref.dtype)


def _attention(proj, cos_t, sin_t, sink_l, *, batch, seq, at_width, kv_width, q_block, k_block, v_block):
    m = proj.shape[0]
    nb = seq // BLOCK
    kv_heads = kv_width // HEAD_DIM
    group = at_width // kv_width
    assert group % 2 == 0 and BLOCK == LANE
    bias = jnp.asarray(_band_bias(group))
    assert nb % 2 == 0
    half = nb // 2
    mid = (2 * BLOCK, lambda b, i: b * half + i)
    prv = (BLOCK, lambda b, i: b * nb + jnp.maximum(2 * i - 1, 0))
    nxt = (BLOCK, lambda b, i: b * nb + jnp.minimum(2 * i + 2, nb - 1))

    def spec(width, where, colblock):
        rows, rowfn = where
        return pl.BlockSpec((rows, width), lambda b, i: (rowfn(b, i), colblock))

    in_specs = [pl.BlockSpec(memory_space=pltpu.SMEM),
                pl.BlockSpec(bias.shape, lambda b, i: (0, 0)), spec(at_width, mid, q_block)]
    in_specs += [spec(kv_width, f, k_block) for f in (prv, mid, nxt)]
    in_specs += [spec(kv_width, f, v_block) for f in (prv, mid, nxt)]
    in_specs += [spec(LANE, f, 0) for f in (prv, mid, nxt)] * 2
    return pl.pallas_call(
        functools.partial(_attn_kernel, nb=nb, group=group, kv_heads=kv_heads),
        grid=(batch, half),
        in_specs=in_specs,
        out_specs=spec(at_width, mid, 0),
        out_shape=jax.ShapeDtypeStruct((m, at_width), _BF16),
        compiler_params=_cparams(("arbitrary", "arbitrary")),
        name="window_attention",
    )(sink_l, bias, proj, proj, proj, proj, proj, proj, proj, cos_t, cos_t, cos_t, sin_t, sin_t, sin_t)


def _silu(g):
    return g * jax.nn.sigmoid(g)


def _outproj_kernel(yf_ref, yb_ref, bv_ref, grw_ref, yat_ref, gat_ref, x_ref, lg_ref, lb_ref, g_ref,
                    wrw_ref, wat_ref, fg_ref, o_ref, mix_ref, *, width, final):
    gmat = g_ref[...]
    inv_n = 1.0 / HEAD_DIM

    def group_mean(z):
        return _bdot(z, gmat) * inv_n

    for s in range(width // SLAB):
        lo, hi = s * SLAB, (s + 1) * SLAB
        y = yf_ref[:, lo:hi].astype(_F32) + yb_ref[:, lo:hi].astype(_F32)
        yc = y - group_mean(y)
        var = group_mean(yc * yc)
        yn = yc * lax.rsqrt(var + LNX_EPS) * lg_ref[:, lo:hi] + lb_ref[:, lo:hi]
        gate = _silu(grw_ref[:, lo:hi].astype(_F32))
        mix_ref[:, lo:hi] = ((yn + bv_ref[:, lo:hi].astype(_F32)) * gate).astype(_BF16)
    mix_at = (yat_ref[...].astype(_F32) * _silu(gat_ref[...].astype(_F32))).astype(_BF16)
    out = x_ref[...] + (jnp.dot(mix_ref[...], wrw_ref[...], preferred_element_type=_F32)
                        + jnp.dot(mix_at, wat_ref[...], preferred_element_type=_F32))
    if final:
        out = out * lax.rsqrt(jnp.mean(out * out, axis=-1, keepdims=True) + NORM_EPS) * fg_ref[...]
    o_ref[...] = out


def _outproj(yf, yb, bv, proj, yat, x2, lnx_g, lnx_b, gmat, w_out, layer, final_g, *, width, at_width,
             grw_block, gat_block, final, tm):
    m, d = x2.shape
    assert width == at_width
    full = lambda shape: pl.BlockSpec(shape, lambda i: (0,) * len(shape))
    in_specs = [
        pl.BlockSpec((tm, width), lambda i: (i, 0)),
        pl.BlockSpec((tm, width), lambda i: (i, 0)),
        pl.BlockSpec((tm, width), lambda i: (i, 0)),
        pl.BlockSpec((tm, width), lambda i: (i, grw_block)),
        pl.BlockSpec((tm, at_width), lambda i: (i, 0)),
        pl.BlockSpec((tm, at_width), lambda i: (i, gat_block)),
        pl.BlockSpec((tm, d), lambda i: (i, 0)),
        full((1, width)), full((1, width)), full((SLAB, SLAB)),
        pl.BlockSpec((None, width, d), lambda i: (layer, 0, 0)),
        pl.BlockSpec((None, at_width, d), lambda i: (layer, 1, 0)), full((1, d)),
    ]
    return pl.pallas_call(
        functools.partial(_outproj_kernel, width=width, final=final),
        grid=(m // tm,),
        in_specs=in_specs,
        out_specs=pl.BlockSpec((tm, d), lambda i: (i, 0)),
        out_shape=jax.ShapeDtypeStruct((m, d), _F32),
        scratch_shapes=[pltpu.VMEM((tm, width), _BF16)],
        compiler_params=_cparams(("arbitrary",)),
        name="gate_outproj",
    )(yf, yb, bv, proj, yat, proj, x2, lnx_g, lnx_b, gmat, w_out, w_out, final_g)


def _tile_plan(m, batch):
    return dict(inproj_tm=min(m, 512), inproj_tn=9 * MXU_DIM, prep_tb=min(m, 256), out_tm=min(m, 256),
                scan_rows=2 if batch % 2 == 0 else 1)


def kernel(x, positions, norm_g, w_in, shift_mu, w0, decay_up, a0, iclr_up, k_k, k_a, r_k, lnx_g, lnx_b,
           sink, w_out, final_g):
    batch, seq, d_model = x.shape
    depth = w_in.shape[0]
    width = k_k.shape[-1]
    at_width = sink.shape[-1] * HEAD_DIM
    rank = decay_up.shape[2]
    in_cols = w_in.shape[-1]
    kv_width = (in_cols - (3 * width + 2 * rank) - width - 2 * at_width) // 2
    m = batch * seq
    assert iclr_up.shape[2] == rank and rank <= RANK_PAD
    assert width % SLAB == 0 and at_width == width and kv_width % LANE == 0
    assert seq % BLOCK == 0 and seq % CHUNK == 0 and BLOCK == WINDOW
    plan = _tile_plan(m, batch)
    assert m % plan["inproj_tm"] == 0 and seq % plan["prep_tb"] == 0 and m % plan["out_tm"] == 0

    o_xw = 3 * width
    o_xa = o_xw + rank
    o_grw = o_xa + rank
    o_q = o_grw + width
    o_k = o_q + at_width
    o_v = o_k + kv_width
    o_gat = o_v + kv_width
    pad_cols = lambda a: jnp.pad(a, [(0, 0)] * (a.ndim - 1) + [(0, RANK_PAD - rank)])
    sources = [(0, o_xw), (o_grw, o_q), (o_q, o_k), (o_gat, in_cols), (o_k, o_v), (o_v, o_gat)]
    sections, dst = [], 0
    for lo, hi in sources:
        sections.append((lo, hi, dst))
        dst += hi - lo
    for lo, hi in ((o_xw, o_xa), (o_xa, o_grw)):
        sections.append((lo, hi, dst))
        dst += RANK_PAD
    n_cols = dst
    w_perm = _relayout_weight(jnp.swapaxes(w_in, 1, 2), tuple(sections), n_cols)
    assert n_cols % plan["inproj_tn"] == 0
    grw_block = (3 * width) // width
    q_block = grw_block + 1
    gat_block = q_block + 1
    kcol = 3 * width + width + 2 * at_width
    assert kcol % kv_width == 0 and (kcol + 2 * kv_width) % (2 * RANK_PAD) == 0
    k_block = kcol // kv_width
    v_block = k_block + 1
    xcol_block = (kcol + 2 * kv_width) // (2 * RANK_PAD)

    mu_rkv = shift_mu[..., :o_xw]
    mu_x = jnp.concatenate([pad_cols(shift_mu[..., o_xw:o_xa]), pad_cols(shift_mu[..., o_xa:o_grw])], axis=-1)
    pad_rows = lambda a: jnp.pad(a, [(0, 0), (0, 0), (0, RANK_PAD - rank), (0, 0)])
    dup = pad_rows(decay_up)
    iup = pad_rows(iclr_up)
    rk_flat = r_k.reshape(depth, 1, width)
    w_out_bf = w_out.astype(_BF16)

    lane_head = np.arange(SLAB) // HEAD_DIM
    gmat = jnp.asarray(lane_head[:, None] == lane_head[None, :], _BF16)
    masks = jnp.asarray(_scan_masks())

    x2 = x.reshape(m, d_model)
    cos_t, sin_t = _rope_tables(positions, m)
    for l in range(depth):
        proj = _inproj(x2, norm_g[l], w_perm, l, plan["inproj_tm"], plan["inproj_tn"])
        r, v, kap, kd, be, lw, bv = _prep(
            proj, mu_rkv[l], mu_x[l], w0[l], dup[l], a0[l], iup[l], k_k[l].reshape(1, width),
            k_a[l].reshape(1, width), rk_flat[l], gmat, seq=seq, width=width, xcol_block=xcol_block,
            tb=plan["prep_tb"])
        yf, yb = _scan(masks, gmat, r, v, kap, kd, be, lw, batch=batch, seq=seq, width=width,
                       rows=plan["scan_rows"])
        yat = _attention(proj, cos_t, sin_t, sink[l], batch=batch, seq=seq, at_width=at_width,
                         kv_width=kv_width, q_block=q_block, k_block=k_block, v_block=v_block)
        x2 = _outproj(yf, yb, bv, proj, yat, x2, lnx_g[l].reshape(1, width), lnx_b[l].reshape(1, width), gmat,
                      w_out_bf, l, final_g.reshape(1, d_model), width=width,
                      at_width=at_width, grw_block=grw_block, gat_block=gat_block,
                      final=(l == depth - 1), tm=plan["out_tm"])
    return x2.reshape(batch, seq, d_model)
```

```python
import functools
import math

import numpy as np
import jax
import jax.numpy as jnp
from jax import lax
from jax.experimental import pallas as pl
from jax.experimental.pallas import tpu as pltpu

HEAD_DIM = 64
WINDOW = 128
BLOCK = 128
ROPE_THETA = 10000.0
NORM_EPS = 1e-6
LNX_EPS = 64e-5
KK_EPS = 1e-24
DECAY_BIAS = 0.5
LOG2E = math.log2(math.e)
EXP_NEG_DECAY_BIAS = math.exp(-DECAY_BIAS)

LANE = 128
SUBLANE = 8
HALO = 2 * SUBLANE
MXU_DIM = 256
VMEM_LIMIT_BYTES = 56 * 1024 * 1024

CHUNK = 64
SLAB = MXU_DIM
RANK_PAD = LANE

_F32 = jnp.float32
_BF16 = jnp.bfloat16
_NN = (((1,), (0,)), ((), ()))
_NT = (((1,), (1,)), ((), ()))
_TN = (((0,), (0,)), ((), ()))


def _bdot(a, b, dims=_NN):
    return lax.dot_general(a.astype(_BF16), b.astype(_BF16), dims, preferred_element_type=_F32)


def _split2(x):
    hi = x.astype(_BF16)
    lo = (x - hi.astype(_F32)).astype(_BF16)
    return hi, lo


def _split3(x):
    hi = x.astype(_BF16)
    r1 = x - hi.astype(_F32)
    mid = r1.astype(_BF16)
    lo = (r1 - mid.astype(_F32)).astype(_BF16)
    return hi, mid, lo


def _dot_exact_rhs(a_bf16, x, parts):
    out = None
    for p in parts(x):
        t = _bdot(a_bf16, p)
        out = t if out is None else out + t
    return out


def _cparams(semantics):
    return pltpu.CompilerParams(dimension_semantics=semantics, vmem_limit_bytes=VMEM_LIMIT_BYTES)


def _rope_table_kernel(pos_ref, inv_ref, cos_ref, sin_ref):
    ang = pos_ref[...].astype(_F32) * inv_ref[...]
    lane = lax.broadcasted_iota(jnp.int32, ang.shape, 1)
    first_half = (lane & (HEAD_DIM - 1)) < (HEAD_DIM // 2)
    s = jnp.sin(ang)
    cos_ref[...] = jnp.cos(ang)
    sin_ref[...] = jnp.where(first_half, -s, s)


def _rope_tables(positions, m):
    half = HEAD_DIM // 2
    inv = jnp.power(ROPE_THETA, -jnp.arange(half, dtype=_F32) / half)
    inv_row = jnp.tile(inv, LANE // half).reshape(1, LANE)
    tb = min(m, 2048)
    return pl.pallas_call(
        _rope_table_kernel,
        grid=(m // tb,),
        in_specs=[pl.BlockSpec((tb, 1), lambda i: (i, 0)), pl.BlockSpec((1, LANE), lambda i: (0, 0))],
        out_specs=[pl.BlockSpec((tb, LANE), lambda i: (i, 0))] * 2,
        out_shape=[jax.ShapeDtypeStruct((m, LANE), _F32)] * 2,
        compiler_params=_cparams(("arbitrary",)),
        name="rope_tables",
    )(positions.reshape(m, 1), inv_row)


def _rope(t, cos, sin):
    w = t.shape[1]
    half = HEAD_DIM // 2
    lane = lax.broadcasted_iota(jnp.int32, t.shape, 1)
    first_half = (lane & (HEAD_DIM - 1)) < half
    partner = jnp.where(first_half, pltpu.roll(t, w - half, 1), pltpu.roll(t, half, 1))
    reps = w // LANE
    return t * jnp.tile(cos, (1, reps)) + partner * jnp.tile(sin, (1, reps))


def _relayout_kernel(w_ref, o_ref, *, sections):
    covered = 0
    for lo, hi, dst in sections:
        if dst > covered:
            o_ref[covered:dst, :] = jnp.zeros((dst - covered, o_ref.shape[1]), o_ref.dtype)
        o_ref[dst:dst + hi - lo, :] = w_ref[lo:hi, :].astype(o_ref.dtype)
        covered = dst + hi - lo
    if covered < o_ref.shape[0]:
        o_ref[covered:, :] = jnp.zeros((o_ref.shape[0] - covered, o_ref.shape[1]), o_ref.dtype)


def _relayout_weight(w_t, sections, n_rows):
    depth, rows, cols = w_t.shape
    tc = min(cols, MXU_DIM)
    return pl.pallas_call(
        functools.partial(_relayout_kernel, sections=sections),
        grid=(depth, cols // tc),
        in_specs=[pl.BlockSpec((None, rows, tc), lambda l, i: (l, 0, i))],
        out_specs=pl.BlockSpec((None, n_rows, tc), lambda l, i: (l, 0, i)),
        out_shape=jax.ShapeDtypeStruct((depth, n_rows, cols), _BF16),
        compiler_params=_cparams(("arbitrary", "arbitrary")),
        name="weight_layout",
    )(w_t)


def _inproj_kernel(x_ref, g_ref, w_ref, o_ref, h_ref, *, tn):
    rows = min(x_ref.shape[0], 256)
    for r0 in range(0, x_ref.shape[0], rows):
        x = x_ref[r0:r0 + rows, :]
        y = x * lax.rsqrt(jnp.mean(x * x, axis=-1, keepdims=True) + NORM_EPS)
        h_ref[r0:r0 + rows, :] = (y * g_ref[...]).astype(_BF16)

    def col_tile(j, carry):
        cols = pl.ds(pl.multiple_of(j * tn, tn), tn)
        tile = lax.dot_general(h_ref[...], w_ref[cols, :], _NT, preferred_element_type=_F32)
        o_ref[:, cols] = tile.astype(o_ref.dtype)
        return carry

    lax.fori_loop(0, w_ref.shape[0] // tn, col_tile, 0)


def _inproj(x2, g, w_t, layer, tm, tn):
    m, d = x2.shape
    n = w_t.shape[1]
    w = w_t
    return pl.pallas_call(
        functools.partial(_inproj_kernel, tn=tn),
        grid=(m // tm,),
        in_specs=[
            pl.BlockSpec((tm, d), lambda i: (i, 0)),
            pl.BlockSpec((1, d), lambda i: (0, 0)),
            pl.BlockSpec((None, n, d), lambda i: (layer, 0, 0), pipeline_mode=pl.Buffered(1)),
        ],
        out_specs=pl.BlockSpec((tm, n), lambda i: (i, 0)),
        out_shape=jax.ShapeDtypeStruct((m, n), _BF16),
        scratch_shapes=[pltpu.VMEM((tm, d), _BF16)],
        compiler_params=_cparams(("arbitrary",)),
        name="norm_inproj",
    )(x2, g.reshape(1, d), w)


def _prep_kernel(rkv_ref, rkvp_ref, rkvn_ref, xw_ref, xwp_ref, xwn_ref, murkv_ref, mux_ref,
                 w0_ref, dup_ref, a0_ref, iup_ref, kk_ref, ka_ref, rk_ref, g_ref,
                 r_out, v_out, kap_out, kd_out, be_out, lw_out, bv_out, *, tb, seq, width):
    i = pl.program_id(0)
    t0 = lax.rem(i * tb, seq)
    first = t0 == 0
    last = t0 + tb == seq
    row = lax.broadcasted_iota(jnp.int32, (tb, 1), 0)
    is_first_row = row == 0
    is_last_row = row == tb - 1
    gmat = g_ref[...]

    def shifted(main, prv, nxt, mu, lo, hi):
        u = main[:, lo:hi].astype(_F32)
        p_row = jnp.where(first, 0.0, prv[HALO - 1:HALO, lo:hi].astype(_F32))
        n_row = jnp.where(last, 0.0, nxt[0:1, lo:hi].astype(_F32))
        up = jnp.where(is_first_row, p_row, pltpu.roll(u, 1, 0))
        un = jnp.where(is_last_row, n_row, pltpu.roll(u, tb - 1, 0))
        mu0, mu1 = mu[0:1, lo:hi], mu[1:2, lo:hi]
        return (1.0 - mu0 - mu1) * u + mu0 * up + mu1 * un

    def group_sum(z):
        return _bdot(z, gmat)

    xs = shifted(xw_ref, xwp_ref, xwn_ref, mux_ref, 0, 2 * RANK_PAD)
    lw_hi, lw_lo = _split2(jnp.tanh(xs[:, :RANK_PAD]))
    xa = xs[:, RANK_PAD:].astype(_BF16)

    for s in range(width // SLAB):
        lo, hi = s * SLAB, (s + 1) * SLAB
        r = shifted(rkv_ref, rkvp_ref, rkvn_ref, murkv_ref, lo, hi)
        k = shifted(rkv_ref, rkvp_ref, rkvn_ref, murkv_ref, width + lo, width + hi)
        v = shifted(rkv_ref, rkvp_ref, rkvn_ref, murkv_ref, 2 * width + lo, 2 * width + hi)
        kkv = k * kk_ref[:, lo:hi]
        ss = group_sum(kkv * kkv)
        kap = kkv * lax.rsqrt(jnp.maximum(ss, KK_EPS))
        kd_sum = None
        for d in range(2):
            dup_hi, dup_lo = _split2(dup_ref[d, :, lo:hi])
            wlin = w0_ref[d:d + 1, lo:hi] + (_bdot(lw_hi, dup_hi) + (_bdot(lw_hi, dup_lo) + _bdot(lw_lo, dup_hi)))
            lw_out[d, :, lo:hi] = -EXP_NEG_DECAY_BIAS * jax.nn.sigmoid(wlin)
            a = jax.nn.sigmoid(a0_ref[d:d + 1, lo:hi] + _bdot(xa, iup_ref[d, :, lo:hi]))
            kd = k * (1.0 + (a - 1.0) * ka_ref[:, lo:hi])
            kd_out[d, :, lo:hi] = kd.astype(kd_out.dtype)
            be_out[d, :, lo:hi] = (a * kap).astype(be_out.dtype)
            kd_sum = kd if kd_sum is None else kd_sum + kd
        bonus = group_sum(r * kd_sum * rk_ref[:, lo:hi])
        r_out[:, lo:hi] = r.astype(r_out.dtype)
        v_out[:, lo:hi] = v.astype(v_out.dtype)
        kap_out[:, lo:hi] = kap.astype(kap_out.dtype)
        bv_out[:, lo:hi] = (bonus * v).astype(bv_out.dtype)


def _prep(proj, mu_rkv, mu_x, w0, dup, a0, iup, k_k, k_a, r_k, gmat, *, seq, width, xcol_block, tb):
    m = proj.shape[0]
    nsub = tb // HALO
    last_sub = m // HALO - 1
    w3 = 3 * width
    full = lambda shape: pl.BlockSpec(shape, lambda i: (0,) * len(shape))
    prev_idx = lambda i: jnp.maximum(i * nsub - 1, 0)
    next_idx = lambda i: jnp.minimum((i + 1) * nsub, last_sub)
    in_specs = [
        pl.BlockSpec((tb, w3), lambda i: (i, 0)),
        pl.BlockSpec((HALO, w3), lambda i: (prev_idx(i), 0)),
        pl.BlockSpec((HALO, w3), lambda i: (next_idx(i), 0)),
        pl.BlockSpec((tb, 2 * RANK_PAD), lambda i: (i, xcol_block)),
        pl.BlockSpec((HALO, 2 * RANK_PAD), lambda i: (prev_idx(i), xcol_block)),
        pl.BlockSpec((HALO, 2 * RANK_PAD), lambda i: (next_idx(i), xcol_block)),
        full((2, w3)), full((2, 2 * RANK_PAD)),
        full((2, width)), full((2, RANK_PAD, width)), full((2, width)), full((2, RANK_PAD, width)),
        full((1, width)), full((1, width)), full((1, width)), full((SLAB, SLAB)),
    ]
    tok = pl.BlockSpec((tb, width), lambda i: (i, 0))
    tok2 = pl.BlockSpec((2, tb, width), lambda i: (0, i, 0))
    sds = jax.ShapeDtypeStruct((m, width), _BF16)
    sds2 = jax.ShapeDtypeStruct((2, m, width), _BF16)
    lw_sds = jax.ShapeDtypeStruct((2, m, width), _F32)
    return pl.pallas_call(
        functools.partial(_prep_kernel, tb=tb, seq=seq, width=width),
        grid=(m // tb,),
        in_specs=in_specs,
        out_specs=[tok, tok, tok, tok2, tok2, tok2, tok],
        out_shape=[sds, sds, sds, sds2, sds2, lw_sds, sds],
        compiler_params=_cparams(("arbitrary",)),
        name="rwkv_prep",
    )(proj, proj, proj, proj, proj, proj, mu_rkv, mu_x, w0, dup, a0, iup, k_k, k_a, r_k, gmat)


_N_LEVELS = int(np.log2(CHUNK)) - 1
_MASK_INCL, _MASK_STRICT, _MASK_BLK2, _MASK_OFF0 = 0, 1, 2, 3
PACK = MXU_DIM // HEAD_DIM
assert CHUNK == HEAD_DIM and LANE == 2 * HEAD_DIM


def _scan_masks():
    ri, cj = np.indices((CHUNK, CHUNK))
    out = np.zeros((2, 3 + _N_LEVELS, CHUNK, CHUNK), np.float32)
    for d in range(2):
        diff = (ri - cj) * (1 - 2 * d)
        out[d, _MASK_INCL] = diff >= 0
        out[d, _MASK_STRICT] = diff > 0
        out[d, _MASK_BLK2] = (ri >> 1) == (cj >> 1)
        for lv in range(_N_LEVELS):
            k = lv + 1
            out[d, _MASK_OFF0 + lv] = ((ri >> (k + 1)) == (cj >> (k + 1))) & ((ri >> k) != (cj >> k))
    return np.tile(out, (1, 1, 1, PACK))


def _scan_kernel(mask_ref, bd_ref, rf_ref, vf_ref, kapf_ref, kdf_ref, bef_ref, lwf_ref,
                 rb_ref, vb_ref, kapb_ref, kdb_ref, beb_ref, lwb_ref, yf_ref, yb_ref, st_ref, *, groups, rows):
    c = CHUNK
    gw = PACK * HEAD_DIM

    @pl.when(pl.program_id(1) == 0)
    def _():
        st_ref[...] = jnp.zeros_like(st_ref)

    blk2 = mask_ref[0, _MASK_BLK2] > 0
    offs = [mask_ref[0, _MASK_OFF0 + lv] > 0 for lv in range(_N_LEVELS)]
    bdm = bd_ref[...]
    bd_keep = bdm > 0

    keep_lo = bdm[0:c, 0:LANE]
    keep_hi = bdm[HEAD_DIM:HEAD_DIM + c, 0:LANE]
    zero = jnp.zeros((c, LANE), _BF16)

    def bd(x):
        xb = x.astype(_BF16)
        blocks = []
        for t in range(gw // LANE):
            col = xb[:, t * LANE:(t + 1) * LANE]
            for keep in (keep_lo, keep_hi):
                row = [zero] * (gw // LANE)
                row[t] = col * keep
                blocks.append(jnp.concatenate(row, axis=1))
        return jnp.concatenate(blocks, axis=0)

    dir_refs =((rf_ref, vf_ref, kapf_ref, kdf_ref, bef_ref, lwf_ref, yf_ref),
                (rb_ref, vb_ref, kapb_ref, kdb_ref, beb_ref, lwb_ref, yb_ref))
    chains = []
    for d, (r_ref, v_ref, kap_ref, kd_ref, be_ref, lw_ref, y_ref) in enumerate(dir_refs):
        incl = mask_ref[d, _MASK_INCL] > 0
        strict = mask_ref[d, _MASK_STRICT] > 0
        tri = mask_ref[d, _MASK_INCL][:, :c].astype(_BF16)
        for b in range(rows):
            lw = lw_ref[b]
            ci = _dot_exact_rhs(tri, lw, _split3)
            ce = ci - lw
            tot = ci[c - 1:c, :] if d == 0 else ci[0:1, :]
            g_inv = jnp.exp(-ci)
            g_tail = jnp.exp(tot - ci)
            kd = kd_ref[b].astype(_F32)
            be = be_ref[b].astype(_F32)
            kt_all = kap_ref[b].astype(_F32) * jnp.exp(ce)
            rt_all = r_ref[b].astype(_F32) * jnp.exp(ci)
            kh_all = kd * g_inv
            bh_all = be * g_inv
            kb_all = kd * g_tail
            bb_all = be * g_tail
            v_all = v_ref[b]
            g_col = jnp.broadcast_to(jnp.exp(tot), (LANE, tot.shape[1])).T
            for g in range(groups):
                s = slice(g * gw, (g + 1) * gw)
                chains.append(dict(
                    incl=incl, strict=strict, kt=kt_all[:, s], rt=rt_all[:, s], kh=kh_all[:, s],
                    bh=bh_all[:, s], kb=kb_all[:, s], bb=bb_all[:, s], v=v_all[:, s],
                    decay=jnp.concatenate([g_col[s, :]] * (gw // LANE), axis=1),
                    y_ref=y_ref, b=b, sl=s, st=(d * rows + b) * groups + g))

    n = range(len(chains))
    ch = chains
    states = [st_ref[ch[i]["st"]] for i in n]
    kr = [jnp.concatenate([ch[i]["kt"], ch[i]["rt"]], axis=0).astype(_BF16) for i in n]
    ks_rs = [_bdot(kr[i], states[i]) for i in n]
    a_k = [_bdot(kr[i], bd(ch[i]["kh"]), _NT) for i in n]
    a_b = [_bdot(kr[i], bd(ch[i]["bh"]), _NT) for i in n]
    a_kk = [jnp.where(ch[i]["strict"], a_k[i][:c], 0.0) for i in n]
    a_rk = [jnp.where(ch[i]["incl"], a_k[i][c:], 0.0) for i in n]
    lmat = [jnp.where(ch[i]["strict"], a_b[i][:c], 0.0) for i in n]
    a_rb = [jnp.where(ch[i]["incl"], a_b[i][c:], 0.0) for i in n]
    pv = [_bdot(jnp.concatenate([a_kk[i], a_rk[i]], axis=0), bd(ch[i]["v"])) for i in n]
    p = [pv[i][:c] for i in n]
    y_kv = [pv[i][c:] for i in n]

    minv = [-jnp.where(blk2, lmat[i], 0.0) for i in n]
    for off in offs:
        loff = [jnp.where(off, lmat[i], 0.0) for i in n]
        q = [loff[i] + _bdot(minv[i], bd(loff[i])) for i in n]
        minv = [minv[i] - q[i] - _bdot(q[i], bd(minv[i])) for i in n]

    z = [ks_rs[i][:c] + p[i] for i in n]
    u = [z[i] + _bdot(minv[i], bd(z[i])) for i in n]
    y_u = [_bdot(a_rb[i], bd(u[i])) for i in n]
    dst = [_bdot(jnp.concatenate([ch[i]["kb"], -ch[i]["bb"]], axis=0),
                 jnp.concatenate([ch[i]["v"].astype(_BF16), u[i].astype(_BF16)], axis=0), _TN) for i in n]

    for i in n:
        ch[i]["y_ref"][ch[i]["b"], :, ch[i]["sl"]] = (ks_rs[i][c:] + y_kv[i] - y_u[i]).astype(yf_ref.dtype)
        st_ref[ch[i]["st"]] = ch[i]["decay"] * states[i] + jnp.where(bd_keep, dst[i], 0.0)


def _scan(masks, bdmask, r, v, kap, kd, be, lw, *, batch, seq, width, rows):
    nc = seq // CHUNK
    gw = PACK * HEAD_DIM
    groups = width // gw
    to3 = lambda a: a.reshape(batch, seq, width)
    to4 = lambda a: a.reshape(2, batch, seq, width)
    fwd = pl.BlockSpec((rows, CHUNK, width), lambda i, c: (i, c, 0))
    bwd = pl.BlockSpec((rows, CHUNK, width), lambda i, c: (i, nc - 1 - c, 0))
    fwd_d = pl.BlockSpec((None, rows, CHUNK, width), lambda i, c: (0, i, c, 0))
    bwd_d = pl.BlockSpec((None, rows, CHUNK, width), lambda i, c: (1, i, nc - 1 - c, 0))
    r3, v3, kap3, kd4, be4, lw4 = to3(r), to3(v), to3(kap), to4(kd), to4(be), to4(lw)
    sds = jax.ShapeDtypeStruct((batch, seq, width), _BF16)
    yf, yb = pl.pallas_call(
        functools.partial(_scan_kernel, groups=groups, rows=rows),
        grid=(batch // rows, nc),
        in_specs=[pl.BlockSpec(masks.shape, lambda i, c: (0, 0, 0, 0)),
                  pl.BlockSpec((gw, gw), lambda i, c: (0, 0)),
                  fwd, fwd, fwd, fwd_d, fwd_d, fwd_d, bwd, bwd, bwd, bwd_d, bwd_d, bwd_d],
        out_specs=[fwd, bwd],
        out_shape=[sds, sds],
        scratch_shapes=[pltpu.VMEM((2 * rows * groups, gw, gw), _F32)],
        compiler_params=_cparams(("arbitrary", "arbitrary")),
        name="rwkv_scan",
    )(masks, bdmask, r3, v3, kap3, kd4, be4, lw4, r3, v3, kap3, kd4, be4, lw4)
    return yf.reshape(batch * seq, width), yb.reshape(batch * seq, width)


def _band_bias(group):
    kj, qi = np.indices((3 * BLOCK, BLOCK))
    band = np.where(np.abs(kj - BLOCK - qi) <= WINDOW, 0.0, -np.inf).astype(np.float32)
    return np.tile(band, (1, group))


def _attn_kernel(sink_ref, bias_ref, q_ref, kp_ref, km_ref, kn_ref, vp_ref, vm_ref, vn_ref,
                 cp_ref, cm_ref, cn_ref, sp_ref, sm_ref, sn_ref, o_ref, *, nb, group, kv_heads):
    i = pl.program_id(1)
    blk = BLOCK
    scale = HEAD_DIM ** -0.5 * LOG2E
    q = _rope(q_ref[...].astype(_F32), cm_ref[...], sm_ref[...]) * scale
    k4 = jnp.concatenate([_rope(kp_ref[...].astype(_F32), cp_ref[...], sp_ref[...]),
                          _rope(km_ref[...].astype(_F32), cm_ref[...], sm_ref[...]),
                          _rope(kn_ref[...].astype(_F32), cn_ref[...], sn_ref[...])], axis=0)
    v4 = jnp.concatenate([vp_ref[...], vm_ref[...], vn_ref[...]], axis=0)

    no_prev = jnp.where(i == 0, -jnp.inf, 0.0)
    no_next = jnp.where(2 * i + 1 == nb - 1, -jnp.inf, 0.0)
    biases = (jnp.concatenate([bias_ref[0:blk, :] + no_prev, bias_ref[blk:, :]], axis=0),
              jnp.concatenate([bias_ref[0:2 * blk, :], bias_ref[2 * blk:, :] + no_next], axis=0))
    head_of_lane = lax.broadcasted_iota(jnp.int32, (1, group * blk), 1) // blk

    chains = [(a, g) for a in range(2) for g in range(kv_heads)]
    sk = []
    for g in range(kv_heads):
        row = jnp.zeros((1, group * blk), _F32)
        for j in range(group):
            row = jnp.where(head_of_lane == j, sink_ref[g * group + j] * LOG2E, row)
        sk.append(row)
    s_t, o_t = [], {}
    for a, g in chains:
        qs = jnp.concatenate([q[a * blk:(a + 1) * blk, (g * group + j) * HEAD_DIM:(g * group + j + 1) * HEAD_DIM]
                              for j in range(group)], axis=0)
        keys = k4[a * blk:(a + 3) * blk, g * HEAD_DIM:(g + 1) * HEAD_DIM]
        s_t.append(_bdot(keys, qs, _NT) + biases[a])
    for c, (a, g) in enumerate(chains):
        mx = jnp.maximum(jnp.max(s_t[c], axis=0, keepdims=True), sk[g])
        p_t = jnp.exp2(s_t[c] - mx)
        denom = jnp.sum(p_t, axis=0, keepdims=True) + jnp.exp2(sk[g] - mx)
        vals = v4[a * blk:(a + 3) * blk, g * HEAD_DIM:(g + 1) * HEAD_DIM]
        o_t[a, g] = _bdot(vals, p_t, _TN) / denom
    for a in range(2):
        for t in range(kv_heads * group // 2):
            g, j = divmod(2 * t, group)
            pair = jnp.concatenate([o_t[a, g][:, j * blk:(j + 1) * blk],
                                    o_t[a, g][:, (j + 1) * blk:(j + 2) * blk]], axis=0)
            o_ref[a * blk:(a + 1) * blk, t * LANE:(t + 1) * LANE] = pair.T.astype(o_ref.dtype)


def _attention(proj, cos_t, sin_t, sink_l, *, batch, seq, at_width, kv_width, q_block, k_block, v_block):
    m = proj.shape[0]
    nb = seq // BLOCK
    kv_heads = kv_width // HEAD_DIM
    group = at_width // kv_width
    assert group % 2 == 0 and BLOCK == LANE
    bias = jnp.asarray(_band_bias(group))
    assert nb % 2 == 0
    half = nb // 2
    mid = (2 * BLOCK, lambda b, i: b * half + i)
    prv = (BLOCK, lambda b, i: b * nb + jnp.maximum(2 * i - 1, 0))
    nxt = (BLOCK, lambda b, i: b * nb + jnp.minimum(2 * i + 2, nb - 1))

    def spec(width, where, colblock):
        rows, rowfn = where
        return pl.BlockSpec((rows, width), lambda b, i: (rowfn(b, i), colblock))

    in_specs = [pl.BlockSpec(memory_space=pltpu.SMEM),
                pl.BlockSpec(bias.shape, lambda b, i: (0, 0)), spec(at_width, mid, q_block)]
    in_specs += [spec(kv_width, f, k_block) for f in (prv, mid, nxt)]
    in_specs += [spec(kv_width, f, v_block) for f in (prv, mid, nxt)]
    in_specs += [spec(LANE, f, 0) for f in (prv, mid, nxt)] * 2
    return pl.pallas_call(
        functools.partial(_attn_kernel, nb=nb, group=group, kv_heads=kv_heads),
        grid=(batch, half),
        in_specs=in_specs,
        out_specs=spec(at_width, mid, 0),
        out_shape=jax.ShapeDtypeStruct((m, at_width), _BF16),
        compiler_params=_cparams(("arbitrary", "arbitrary")),
        name="window_attention",
    )(sink_l, bias, proj, proj, proj, proj, proj, proj, proj, cos_t, cos_t, cos_t, sin_t, sin_t, sin_t)


def _silu(g):
    return g * jax.nn.sigmoid(g)


def _outproj_kernel(yf_ref, yb_ref, bv_ref, grw_ref, yat_ref, gat_ref, x_ref, lg_ref, lb_ref, g_ref,
                    wrw_ref, wat_ref, fg_ref, o_ref, mix_ref, *, width, final):
    gmat = g_ref[...]
    inv_n = 1.0 / HEAD_DIM

    def group_mean(z):
        return _bdot(z, gmat) * inv_n

    for s in range(width // SLAB):
        lo, hi = s * SLAB, (s + 1) * SLAB
        y = yf_ref[:, lo:hi].astype(_F32) + yb_ref[:, lo:hi].astype(_F32)
        yc = y - group_mean(y)
        var = group_mean(yc * yc)
        yn = yc * lax.rsqrt(var + LNX_EPS) * lg_ref[:, lo:hi] + lb_ref[:, lo:hi]
        gate = _silu(grw_ref[:, lo:hi].astype(_F32))
        mix_ref[:, lo:hi] = ((yn + bv_ref[:, lo:hi].astype(_F32)) * gate).astype(_BF16)
    mix_at = (yat_ref[...].astype(_F32) * _silu(gat_ref[...].astype(_F32))).astype(_BF16)
    out = x_ref[...] + (jnp.dot(mix_ref[...], wrw_ref[...], preferred_element_type=_F32)
                        + jnp.dot(mix_at, wat_ref[...], preferred_element_type=_F32))
    if final:
        out = out * lax.rsqrt(jnp.mean(out * out, axis=-1, keepdims=True) + NORM_EPS) * fg_ref[...]
    o_ref[...] = out


def _outproj(yf, yb, bv, proj, yat, x2, lnx_g, lnx_b, gmat, w_out, layer, final_g, *, width, at_width,
             grw_block, gat_block, final, tm):
    m, d = x2.shape
    assert width == at_width
    full = lambda shape: pl.BlockSpec(shape, lambda i: (0,) * len(shape))
    in_specs = [
        pl.BlockSpec((tm, width), lambda i: (i, 0)),
        pl.BlockSpec((tm, width), lambda i: (i, 0)),
        pl.BlockSpec((tm, width), lambda i: (i, 0)),
        pl.BlockSpec((tm, width), lambda i: (i, grw_block)),
        pl.BlockSpec((tm, at_width), lambda i: (i, 0)),
        pl.BlockSpec((tm, at_width), lambda i: (i, gat_block)),
        pl.BlockSpec((tm, d), lambda i: (i, 0)),
        full((1, width)), full((1, width)), full((SLAB, SLAB)),
        pl.BlockSpec((None, width, d), lambda i: (layer, 0, 0)),
        pl.BlockSpec((None, at_width, d), lambda i: (layer, 1, 0)), full((1, d)),
    ]
    return pl.pallas_call(
        functools.partial(_outproj_kernel, width=width, final=final),
        grid=(m // tm,),
        in_specs=in_specs,
        out_specs=pl.BlockSpec((tm, d), lambda i: (i, 0)),
        out_shape=jax.ShapeDtypeStruct((m, d), _F32),
        scratch_shapes=[pltpu.VMEM((tm, width), _BF16)],
        compiler_params=_cparams(("arbitrary",)),
        name="gate_outproj",
    )(yf, yb, bv, proj, yat, proj, x2, lnx_g, lnx_b, gmat, w_out, w_out, final_g)


def _tile_plan(m, batch):
    return dict(inproj_tm=min(m, 512), inproj_tn=9 * MXU_DIM, prep_tb=min(m, 512), out_tm=min(m, 256),
                scan_rows=4 if batch % 4 == 0 else 2 if batch % 2 == 0 else 1)


def kernel(x, positions, norm_g, w_in, shift_mu, w0, decay_up, a0, iclr_up, k_k, k_a, r_k, lnx_g, lnx_b,
           sink, w_out, final_g):
    batch, seq, d_model = x.shape
    depth = w_in.shape[0]
    width = k_k.shape[-1]
    at_width = sink.shape[-1] * HEAD_DIM
    rank = decay_up.shape[2]
    in_cols = w_in.shape[-1]
    kv_width = (in_cols - (3 * width + 2 * rank) - width - 2 * at_width) // 2
    m = batch * seq
    assert iclr_up.shape[2] == rank and rank <= RANK_PAD
    assert width % SLAB == 0 and at_width == width and kv_width % LANE == 0
    assert seq % BLOCK == 0 and seq % CHUNK == 0 and BLOCK == WINDOW
    plan = _tile_plan(m, batch)
    assert m % plan["inproj_tm"] == 0 and seq % plan["prep_tb"] == 0 and m % plan["out_tm"] == 0

    o_xw = 3 * width
    o_xa = o_xw + rank
    o_grw = o_xa + rank
    o_q = o_grw + width
    o_k = o_q + at_width
    o_v = o_k + kv_width
    o_gat = o_v + kv_width
    pad_cols = lambda a: jnp.pad(a, [(0, 0)] * (a.ndim - 1) + [(0, RANK_PAD - rank)])
    sources = [(0, o_xw), (o_grw, o_q), (o_q, o_k), (o_gat, in_cols), (o_k, o_v), (o_v, o_gat)]
    sections, dst = [], 0
    for lo, hi in sources:
        sections.append((lo, hi, dst))
        dst += hi - lo
    for lo, hi in ((o_xw, o_xa), (o_xa, o_grw)):
        sections.append((lo, hi, dst))
        dst += RANK_PAD
    n_cols = dst
    w_perm = _relayout_weight(jnp.swapaxes(w_in, 1, 2), tuple(sections), n_cols)
    assert n_cols % plan["inproj_tn"] == 0
    grw_block = (3 * width) // width
    q_block = grw_block + 1
    gat_block = q_block + 1
    kcol = 3 * width + width + 2 * at_width
    assert kcol % kv_width == 0 and (kcol + 2 * kv_width) % (2 * RANK_PAD) == 0
    k_block = kcol // kv_width
    v_block = k_block + 1
    xcol_block = (kcol + 2 * kv_width) // (2 * RANK_PAD)

    mu_rkv = shift_mu[..., :o_xw]
    mu_x = jnp.concatenate([pad_cols(shift_mu[..., o_xw:o_xa]), pad_cols(shift_mu[..., o_xa:o_grw])], axis=-1)
    pad_rows = lambda a: jnp.pad(a, [(0, 0), (0, 0), (0, RANK_PAD - rank), (0, 0)])
    dup = pad_rows(decay_up)
    iup = pad_rows(iclr_up)
    rk_flat = r_k.reshape(depth, 1, width)
    w_out_bf = w_out.astype(_BF16)

    lane_head = np.arange(SLAB) // HEAD_DIM
    gmat = jnp.asarray(lane_head[:, None] == lane_head[None, :], _BF16)
    masks = jnp.asarray(_scan_masks())

    x2 = x.reshape(m, d_model)
    cos_t, sin_t = _rope_tables(positions, m)
    for l in range(depth):
        proj = _inproj(x2, norm_g[l], w_perm, l, plan["inproj_tm"], plan["inproj_tn"])
        r, v, kap, kd, be, lw, bv = _prep(
            proj, mu_rkv[l], mu_x[l], w0[l], dup[l], a0[l], iup[l], k_k[l].reshape(1, width),
            k_a[l].reshape(1, width), rk_flat[l], gmat, seq=seq, width=width, xcol_block=xcol_block,
            tb=plan["prep_tb"])
        yf, yb = _scan(masks, gmat, r, v, kap, kd, be, lw, batch=batch, seq=seq, width=width,
                       rows=plan["scan_rows"])
        yat = _attention(proj, cos_t, sin_t, sink[l], batch=batch, seq=seq, at_width=at_width,
                         kv_width=kv_width, q_block=q_block, k_block=k_block, v_block=v_block)
        x2 = _outproj(yf, yb, bv, proj, yat, x2, lnx_g[l].reshape(1, width), lnx_b[l].reshape(1, width), gmat,
                      w_out_bf, l, final_g.reshape(1, d_model), width=width,
                      at_width=at_width, grw_block=grw_block, gat_block=gat_block,
                      final=(l == depth - 1), tm=plan["out_tm"])
    return x2.reshape(batch, seq, d_model)
```
